```python
import math, functools
import jax, jax.numpy as jnp
from jax import lax
import numpy as np

D_MODEL = 2048
BATCH = 1
SEQ = 8192
DEPTH = 2
DEC_BATCH = 128
DEC_SEQ = 1
PAST_LEN = 2048
PAGE_SIZE = 128

SSD_EXPAND = 2
D_SSD = SSD_EXPAND * D_MODEL
SSD_HEADDIM = 64
SSD_HEADS = D_SSD // SSD_HEADDIM
SSD_GROUPS = 8
SSD_STATE = 128
SSD_CONV = 4
SSD_CHUNK = 256
CONV_CH = D_SSD + 2 * SSD_GROUPS * SSD_STATE
DT_MIN = 0.001
DT_MAX = 0.1
NSA_HEADS = 16
NSA_KV_HEADS = 4
HEAD_DIM = 128
Q_PER_KV = NSA_HEADS // NSA_KV_HEADS
CMP_STRIDE = 16
CMP_BLOCK = 2 * CMP_STRIDE
CMP_HIDDEN = HEAD_DIM
SEL_BLOCK = 64
N_SEL = 16
WINDOW = 512
Q_BLOCK = 128
FORCED_SCORE = 1e4
REL_BUCKETS = 32
REL_MAX_DIST = 128
MOE_GROUPS = 4
MOE_PER_GROUP = 4
N_EXPERTS = MOE_GROUPS * MOE_PER_GROUP
MOE_TOPK = 2
D_EXPERT = D_MODEL // 4
RMS_EPS = 1e-6
KV_COLS = 2 * NSA_KV_HEADS * HEAD_DIM
IN_SIZES = (D_SSD, CONV_CH, SSD_HEADS, NSA_HEADS * HEAD_DIM, KV_COLS, KV_COLS, KV_COLS, 3 * NSA_HEADS, 2 * D_MODEL)
D_IN = D_SSD + CONV_CH + SSD_HEADS + NSA_HEADS * HEAD_DIM + 3 * KV_COLS + 3 * NSA_HEADS + 2 * D_MODEL

kernel_name = 'hybrid_ssd_nsa_hmoe_step'


def rmsnorm(x, w):
    xf = x.astype(jnp.float32)
    y = xf * lax.rsqrt(jnp.mean(xf * xf, axis=-1, keepdims=True) + RMS_EPS)
    return (y * w.astype(jnp.float32)).astype(x.dtype)


def split_cols(a, sizes):
    out, off = [], 0
    for s in sizes:
        out.append(a[..., off:off + s])
        off += s
    return out


def ssd_scan(xdt, dt, A, bm, cm, h0):
    bsz, L, H, P = xdt.shape
    G, N = bm.shape[2], bm.shape[3]
    J = H // G
    Q = min(SSD_CHUNK, L)
    nc = -(-L // Q)
    pad = nc * Q - L

    def chunks(a):
        a = jnp.pad(a.astype(jnp.float32), [(0, 0), (0, pad)] + [(0, 0)] * (a.ndim - 2))
        return jnp.moveaxis(a.reshape((bsz, nc, Q) + a.shape[2:]), 1, 0)

    tri = jnp.tril(jnp.ones((Q, Q), dtype=bool))

    def step(h, inp):
        xc, dc, bc, cc = inp
        cum = jnp.cumsum(dc * A, axis=1)
        cum_t = jnp.moveaxis(cum, 1, 2)
        seg = jnp.exp(jnp.where(tri, cum_t[..., :, None] - cum_t[..., None, :], -jnp.inf))
        att = jnp.einsum('btgn,bsgn->bgts', cc, bc)[:, :, None] * seg.reshape(bsz, G, J, Q, Q)
        xg = xc.reshape(bsz, Q, G, J, P)
        hg = h.reshape(bsz, G, J, P, N)
        y = jnp.einsum('bgjts,bsgjp->btgjp', att, xg)
        y = y + jnp.einsum('btgn,bgjpn->btgjp', cc, hg) * jnp.exp(cum).reshape(bsz, Q, G, J)[..., None]
        tail = jnp.exp(cum[:, -1:, :] - cum).reshape(bsz, Q, G, J)
        hg = (hg * jnp.exp(cum[:, -1]).reshape(bsz, G, J)[..., None, None]
              + jnp.einsum('bsgn,bsgjp->bgjpn', bc, xg * tail[..., None]))
        return hg.reshape(bsz, H, P, N), y.reshape(bsz, Q, H, P)

    h_last, ys = lax.scan(step, h0.astype(jnp.float32), (chunks(xdt), chunks(dt), chunks(bm), chunks(cm)))
    y = jnp.moveaxis(ys, 0, 1).reshape(bsz, nc * Q, H, P)[:, :L]
    return y, h_last


def ssd_branch(z, xbc, dtr, conv_prev, h0, p):
    bsz, L, _ = xbc.shape
    xp = jnp.concatenate([conv_prev.astype(xbc.dtype), xbc], axis=1)
    conv = p['conv_b'] + sum(xp[:, k:k + L] * p['conv_w'][k] for k in range(SSD_CONV))
    xs, bm, cm = split_cols(jax.nn.silu(conv), (D_SSD, SSD_GROUPS * SSD_STATE, SSD_GROUPS * SSD_STATE))
    xs = xs.reshape(bsz, L, SSD_HEADS, SSD_HEADDIM).astype(jnp.float32)
    dt = jax.nn.softplus(dtr.astype(jnp.float32) + p['dt_bias'].astype(jnp.float32))
    A = -jnp.exp(p['a_log'].astype(jnp.float32))
    y, h_last = ssd_scan(xs * dt[..., None], dt, A,
                         bm.reshape(bsz, L, SSD_GROUPS, SSD_STATE), cm.reshape(bsz, L, SSD_GROUPS, SSD_STATE), h0)
    y = y + p['d_skip'].astype(jnp.float32)[:, None] * xs
    y = rmsnorm(y.reshape(bsz, L, D_SSD) * jax.nn.silu(z.astype(jnp.float32)), p['ssd_norm_w']).astype(z.dtype)
    return y @ p['w_ssd_out'], h_last.astype(h0.dtype), xp[:, L:]


def rel_bucket(dist):
    n = jnp.maximum(dist, 0)
    exact = REL_BUCKETS // 2
    nf = jnp.maximum(n, 1).astype(jnp.float32)
    big = exact + (jnp.log(nf / exact) / math.log(REL_MAX_DIST / exact) * (REL_BUCKETS - exact)).astype(jnp.int32)
    return jnp.where(n < exact, n, jnp.minimum(big, REL_BUCKETS - 1))


def masked_softmax(s, mask):
    s = jnp.where(mask, s.astype(jnp.float32), -1e30)
    e = jnp.where(mask, jnp.exp(s - jnp.max(s, axis=-1, keepdims=True)), 0.0)
    return e / jnp.maximum(jnp.sum(e, axis=-1, keepdims=True), 1e-30)


def cmp_weights(p):
    w1h = p['cmp_w1'].reshape(2, 2, CMP_STRIDE, HEAD_DIM, CMP_HIDDEN)
    b1c = p['cmp_b1'] + jnp.einsum('srd,srdf->sf', p['cmp_pe'], p['cmp_w1'])
    return w1h, b1c


def chunk_proj(kv, w1h):
    bsz, L = kv.shape[:2]
    ch = kv.reshape(bsz, L // CMP_STRIDE, CMP_STRIDE, NSA_KV_HEADS, 2, HEAD_DIM)
    return jnp.einsum('bnrhsd,sqrdf->bnhsqf', ch, w1h)


def compress(P, b1c, w2):
    hid = jax.nn.silu(P[:, :-1, :, :, 0] + P[:, 1:, :, :, 1] + b1c)
    out = jnp.einsum('bnhsf,sfd->bnhsd', hid, w2)
    return out[..., 0, :], out[..., 1, :]


def cmp_to_sel_weights(nc, ns):
    c0 = jnp.arange(nc)[:, None] * CMP_STRIDE
    s0 = jnp.arange(ns)[None, :] * SEL_BLOCK
    inter = jnp.minimum(c0 + CMP_BLOCK, s0 + SEL_BLOCK) - jnp.maximum(c0, s0)
    return jnp.maximum(inter, 0).astype(jnp.float32) / CMP_BLOCK


def nsa_attend(q, q_pos, kc, vc, c_end, sel_fn, n_blocks, kw, k_pos, gates, rel_bias):
    scale = HEAD_DIM ** -0.5
    rb = rel_bias.reshape(REL_BUCKETS, NSA_KV_HEADS, Q_PER_KV)
    hi = jnp.arange(NSA_KV_HEADS)[None, None, :, None]
    dist = q_pos[:, None] - c_end[None, :]
    s = jnp.einsum('blgjd,bngd->blgjn', q, kc) * scale + jnp.moveaxis(rb[rel_bucket(dist)], 1, -1)
    p_c = masked_softmax(s, (dist >= 0)[:, None, None, :])
    o_c = jnp.einsum('blgjn,bngd->blgjd', p_c, vc)
    imp = jnp.einsum('blgjn,nm->blgm', p_c, cmp_to_sel_weights(kc.shape[1], n_blocks))
    cur = (q_pos // SEL_BLOCK)[:, None]
    blk = jnp.arange(n_blocks)[None, :]
    forced = (blk == 0) | (blk == cur) | (blk == cur - 1)
    score = jnp.where(forced[None, :, None, :], FORCED_SCORE,
                      jnp.where((blk <= cur)[None, :, None, :], imp, -1.0))
    _, idx = lax.top_k(score, min(N_SEL, n_blocks))
    pos = (idx[..., None] * SEL_BLOCK + jnp.arange(SEL_BLOCK)).reshape(idx.shape[:3] + (-1,))
    kv_s = sel_fn(pos)
    dist_s = q_pos[None, :, None, None] - pos
    s = (jnp.einsum('blgjd,blgkd->blgjk', q, kv_s[..., 0, :]) * scale
         + jnp.moveaxis(rb[rel_bucket(dist_s), hi], -1, 3))
    p_s = masked_softmax(s, (dist_s >= 0)[:, :, :, None, :])
    o_s = jnp.einsum('blgjk,blgkd->blgjd', p_s, kv_s[..., 1, :])
    dist_w = q_pos[:, None] - k_pos[None, :]
    valid_w = (dist_w >= 0) & (dist_w < WINDOW) & (k_pos >= 0)[None, :]
    s = jnp.einsum('blgjd,bkgd->blgjk', q, kw[..., 0, :]) * scale + jnp.moveaxis(rb[rel_bucket(dist_w)], 1, -1)
    p_w = masked_softmax(s, valid_w[:, None, None, :])
    o_w = jnp.einsum('blgjk,bkgd->blgjd', p_w, kw[..., 1, :])
    g = gates[..., None]
    out = g[:, :, 0] * o_c + g[:, :, 1] * o_s + g[:, :, 2] * o_w
    return out.astype(q.dtype)


def nsa_prompt(q, kvc, kvs, kvw, gn, p, rel_bias):
    bsz, S = q.shape[:2]
    q = q.reshape(bsz, S, NSA_KV_HEADS, Q_PER_KV, HEAD_DIM)
    kvc, kvs, kvw = [a.reshape(bsz, S, NSA_KV_HEADS, 2, HEAD_DIM) for a in (kvc, kvs, kvw)]
    gates = jax.nn.sigmoid(gn.astype(jnp.float32)).reshape(bsz, S, 3, NSA_KV_HEADS, Q_PER_KV)
    w1h, b1c = cmp_weights(p)
    kc, vc = compress(chunk_proj(kvc, w1h), b1c, p['cmp_w2'])
    c_end = jnp.arange(kc.shape[1]) * CMP_STRIDE + CMP_BLOCK - 1
    n_blocks = -(-S // SEL_BLOCK)
    bi = jnp.arange(bsz)[:, None, None, None]
    hi = jnp.arange(NSA_KV_HEADS)[None, None, :, None]

    def sel_fn(pos):
        return kvs[bi, jnp.clip(pos, 0, S - 1), hi]

    kw_pad = jnp.pad(kvw, ((0, 0), (WINDOW, 0), (0, 0), (0, 0), (0, 0)))

    def block(i):
        t0 = i * Q_BLOCK
        q_pos = t0 + jnp.arange(Q_BLOCK)
        k_pos = t0 - WINDOW + jnp.arange(Q_BLOCK + WINDOW)
        return nsa_attend(lax.dynamic_slice_in_dim(q, t0, Q_BLOCK, 1), q_pos, kc, vc, c_end, sel_fn, n_blocks,
                          lax.dynamic_slice_in_dim(kw_pad, t0, Q_BLOCK + WINDOW, 1), k_pos,
                          lax.dynamic_slice_in_dim(gates, t0, Q_BLOCK, 1), rel_bias)

    out = lax.map(block, jnp.arange(S // Q_BLOCK))
    out = jnp.moveaxis(out, 0, 1).reshape(bsz, S, NSA_HEADS * HEAD_DIM)
    wc = min(WINDOW, S)
    return out, (kvc, kvs, kvw[:, S - wc:])


def nsa_sample(q, kvc, kvs, kvw, gn, p, rel_bias, cache_cmp_kv, cache_sel_kv, win_buf, page_table, layer):
    bsz, Lq = q.shape[:2]
    past = page_table.shape[1] * PAGE_SIZE
    q = q.reshape(bsz, Lq, NSA_KV_HEADS, Q_PER_KV, HEAD_DIM)
    kvc, kvs, kvw = [a.reshape(bsz, Lq, NSA_KV_HEADS, 2, HEAD_DIM) for a in (kvc, kvs, kvw)]
    gates = jax.nn.sigmoid(gn.astype(jnp.float32)).reshape(bsz, Lq, 3, NSA_KV_HEADS, Q_PER_KV)
    w1h, b1c = cmp_weights(p)
    past_c = cache_cmp_kv[layer, page_table].reshape(bsz, past, NSA_KV_HEADS, 2, HEAD_DIM)
    n_new = (Lq // CMP_STRIDE) * CMP_STRIDE
    P = jnp.concatenate([chunk_proj(past_c.astype(kvc.dtype), w1h), chunk_proj(kvc[:, :n_new], w1h)], axis=1)
    kc, vc = compress(P, b1c, p['cmp_w2'])
    c_end = jnp.arange(kc.shape[1]) * CMP_STRIDE + CMP_BLOCK - 1
    n_blocks = -(-(past + Lq) // SEL_BLOCK)
    bi = jnp.arange(bsz)[:, None, None, None]
    hi = jnp.arange(NSA_KV_HEADS)[None, None, :, None]

    def sel_fn(pos):
        pc = jnp.clip(pos, 0, past - 1)
        phys = page_table[bi, pc // PAGE_SIZE]
        old = cache_sel_kv[layer, phys, pc % PAGE_SIZE, hi].astype(kvs.dtype)
        new = kvs[bi, jnp.clip(pos - past, 0, Lq - 1), hi]
        return jnp.where((pos < past)[..., None, None], old, new)

    wc = win_buf.shape[1]
    cat_w = jnp.concatenate([win_buf.astype(kvw.dtype), kvw], axis=1)
    q_pos = past + jnp.arange(Lq)
    k_pos = past - wc + jnp.arange(wc + Lq)
    out = nsa_attend(q, q_pos, kc, vc, c_end, sel_fn, n_blocks, cat_w, k_pos, gates, rel_bias)
    keep = min(WINDOW, wc + Lq)
    return out.reshape(bsz, Lq, NSA_HEADS * HEAD_DIM), (kvc, kvs, cat_w[:, wc + Lq - keep:])


def moe_ffn(h, p):
    bsz, L, _ = h.shape
    pg = jax.nn.softmax((h @ p['w_router_group'] + p['b_router_group']).astype(jnp.float32), axis=-1)
    p_top, g_top = lax.top_k(pg, 1)
    le = (h @ p['w_router_exp'] + p['b_router_exp']).astype(jnp.float32).reshape(bsz, L, MOE_GROUPS, MOE_PER_GROUP)
    le = jnp.einsum('blge,blg->ble', le, jax.nn.one_hot(g_top[..., 0], MOE_GROUPS, dtype=jnp.float32))
    w2, e2 = lax.top_k(jax.nn.softmax(le, axis=-1), MOE_TOPK)
    w2 = w2 / jnp.sum(w2, axis=-1, keepdims=True) * p_top
    eid = g_top * MOE_PER_GROUP + e2
    gate = jnp.einsum('blk,blke->ble', w2, jax.nn.one_hot(eid, N_EXPERTS, dtype=jnp.float32))
    u = jnp.einsum('bld,edf->blef', h, p['w_exp_in'])
    a, b = u[..., :D_EXPERT], u[..., D_EXPERT:]
    act = jax.nn.silu(a) * b * gate[..., None].astype(h.dtype)
    return jnp.einsum('blef,efd->bld', act, p['w_exp_out'])


def decoder_layer(x, p, conv_prev, h0, nsa_fn):
    h = rmsnorm(x, p['norm1_w'])
    z, xbc, dtr, q, kvc, kvs, kvw, gn, gm = split_cols(h @ p['w_in'], IN_SIZES)
    y_a, h_last, conv_last = ssd_branch(z, xbc, dtr, conv_prev, h0, p)
    attn, (kvc_new, kvs_new, win_new) = nsa_fn(q, kvc, kvs, kvw, gn)
    y_b = attn @ p['w_nsa_out']
    g = jax.nn.sigmoid(gm.astype(jnp.float32)).reshape(x.shape[0], x.shape[1], 2, D_MODEL)
    m = (g[..., 0, :] * y_a + g[..., 1, :] * y_b).astype(x.dtype)
    x = x + m @ p['w_out']
    x = x + moe_ffn(rmsnorm(x, p['norm2_w']), p)
    return x, (kvc_new, kvs_new, win_new, h_last, conv_last)


def setup_inputs(seed: int = 0) -> dict:
    key = jax.random.key(seed)
    k = jax.random.split(key, 32)
    f32 = jnp.float32
    n_pages = PAST_LEN // PAGE_SIZE
    n_pool = (DEC_BATCH * n_pages * 5) // 4
    win = min(WINDOW, PAST_LEN)
    nrm = lambda kk, shape, sc: jax.random.normal(kk, shape, f32) * sc
    page_table = jax.random.permutation(k[7], n_pool)[:DEC_BATCH * n_pages].reshape(DEC_BATCH, n_pages).astype(jnp.int32)
    dt0 = jnp.exp(jax.random.uniform(k[15], (DEPTH, SSD_HEADS)) * (math.log(DT_MAX) - math.log(DT_MIN)) + math.log(DT_MIN))
    return {
        'x_prompt': nrm(k[0], (BATCH, SEQ, D_MODEL), 1.0),
        'x_sample': nrm(k[1], (DEC_BATCH, DEC_SEQ, D_MODEL), 1.0),
        'cache_cmp_kv': nrm(k[2], (DEPTH, n_pool, PAGE_SIZE, NSA_KV_HEADS, 2, HEAD_DIM), 1.0),
        'cache_sel_kv': nrm(k[3], (DEPTH, n_pool, PAGE_SIZE, NSA_KV_HEADS, 2, HEAD_DIM), 1.0),
        'cache_win_kv': nrm(k[4], (DEPTH, DEC_BATCH, win, NSA_KV_HEADS, 2, HEAD_DIM), 1.0),
        'state_ssm': nrm(k[5], (DEPTH, DEC_BATCH, SSD_HEADS, SSD_HEADDIM, SSD_STATE), 0.1),
        'state_conv': nrm(k[6], (DEPTH, DEC_BATCH, SSD_CONV - 1, CONV_CH), 1.0),
        'page_table': page_table,
        'rel_bias': nrm(k[8], (REL_BUCKETS, NSA_HEADS), 0.5),
        'norm1_w': 1.0 + nrm(k[9], (DEPTH, D_MODEL), 0.02),
        'norm2_w': 1.0 + nrm(k[10], (DEPTH, D_MODEL), 0.02),
        'final_norm_w': 1.0 + nrm(k[11], (D_MODEL,), 0.02),
        'w_in': nrm(k[12], (DEPTH, D_MODEL, D_IN), D_MODEL ** -0.5),
        'conv_w': nrm(k[13], (DEPTH, SSD_CONV, CONV_CH), SSD_CONV ** -0.5),
        'conv_b': nrm(k[14], (DEPTH, CONV_CH), 0.01),
        'dt_bias': dt0 + jnp.log(-jnp.expm1(-dt0)),
        'a_log': jnp.log(jax.random.uniform(k[16], (DEPTH, SSD_HEADS), minval=1.0, maxval=16.0)),
        'd_skip': 1.0 + nrm(k[17], (DEPTH, SSD_HEADS), 0.02),
        'ssd_norm_w': 1.0 + nrm(k[18], (DEPTH, D_SSD), 0.02),
        'w_ssd_out': nrm(k[19], (DEPTH, D_SSD, D_MODEL), D_SSD ** -0.5),
        'cmp_pe': nrm(k[20], (DEPTH, 2, CMP_BLOCK, HEAD_DIM), 0.5),
        'cmp_w1': nrm(k[21], (DEPTH, 2, CMP_BLOCK, HEAD_DIM, CMP_HIDDEN), (CMP_BLOCK * HEAD_DIM) ** -0.5),
        'cmp_b1': nrm(k[22], (DEPTH, 2, CMP_HIDDEN), 0.01),
        'cmp_w2': nrm(k[23], (DEPTH, 2, CMP_HIDDEN, HEAD_DIM), CMP_HIDDEN ** -0.5),
        'w_nsa_out': nrm(k[24], (DEPTH, NSA_HEADS * HEAD_DIM, D_MODEL), (NSA_HEADS * HEAD_DIM) ** -0.5),
        'w_out': nrm(k[25], (DEPTH, D_MODEL, D_MODEL), D_MODEL ** -0.5),
        'w_router_group': nrm(k[26], (DEPTH, D_MODEL, MOE_GROUPS), D_MODEL ** -0.5),
        'b_router_group': nrm(k[27], (DEPTH, MOE_GROUPS), 0.01),
        'w_router_exp': nrm(k[28], (DEPTH, D_MODEL, N_EXPERTS), D_MODEL ** -0.5),
        'b_router_exp': nrm(k[29], (DEPTH, N_EXPERTS), 0.01),
        'w_exp_in': nrm(k[30], (DEPTH, N_EXPERTS, D_MODEL, 2 * D_EXPERT), D_MODEL ** -0.5),
        'w_exp_out': nrm(k[31], (DEPTH, N_EXPERTS, D_EXPERT, D_MODEL), D_EXPERT ** -0.5),
    }


def reference(x_prompt, x_sample, cache_cmp_kv, cache_sel_kv, cache_win_kv, state_ssm, state_conv, page_table,
              rel_bias, norm1_w, norm2_w, final_norm_w, w_in, conv_w, conv_b, dt_bias, a_log, d_skip, ssd_norm_w,
              w_ssd_out, cmp_pe, cmp_w1, cmp_b1, cmp_w2, w_nsa_out, w_out, w_router_group, b_router_group,
              w_router_exp, b_router_exp, w_exp_in, w_exp_out):
    xp, xs = x_prompt, x_sample
    cmp_p, cmp_s, sel_p, sel_s, win_p, win_s, ssm_p, ssm_s, conv_p, conv_s = ([] for _ in range(10))
    for l in range(DEPTH):
        p = {'norm1_w': norm1_w[l], 'norm2_w': norm2_w[l], 'w_in': w_in[l], 'conv_w': conv_w[l], 'conv_b': conv_b[l],
             'dt_bias': dt_bias[l], 'a_log': a_log[l], 'd_skip': d_skip[l], 'ssd_norm_w': ssd_norm_w[l],
             'w_ssd_out': w_ssd_out[l], 'cmp_pe': cmp_pe[l], 'cmp_w1': cmp_w1[l], 'cmp_b1': cmp_b1[l],
             'cmp_w2': cmp_w2[l], 'w_nsa_out': w_nsa_out[l], 'w_out': w_out[l],
             'w_router_group': w_router_group[l], 'b_router_group': b_router_group[l],
             'w_router_exp': w_router_exp[l], 'b_router_exp': b_router_exp[l],
             'w_exp_in': w_exp_in[l], 'w_exp_out': w_exp_out[l]}
        conv0 = jnp.zeros((xp.shape[0], SSD_CONV - 1, CONV_CH), xp.dtype)
        h00 = jnp.zeros((xp.shape[0], SSD_HEADS, SSD_HEADDIM, SSD_STATE), xp.dtype)
        xp, st = decoder_layer(xp, p, conv0, h00, functools.partial(nsa_prompt, p=p, rel_bias=rel_bias))
        cmp_p.append(st[0]); sel_p.append(st[1]); win_p.append(st[2]); ssm_p.append(st[3]); conv_p.append(st[4])
        nsa_s = functools.partial(nsa_sample, p=p, rel_bias=rel_bias, cache_cmp_kv=cache_cmp_kv,
                                  cache_sel_kv=cache_sel_kv, win_buf=cache_win_kv[l], page_table=page_table, layer=l)
        xs, st = decoder_layer(xs, p, state_conv[l], state_ssm[l], nsa_s)
        cmp_s.append(st[0]); sel_s.append(st[1]); win_s.append(st[2]); ssm_s.append(st[3]); conv_s.append(st[4])
    y_prompt = rmsnorm(xp, final_norm_w)
    y_sample = rmsnorm(xs, final_norm_w)
    return (y_prompt, y_sample, jnp.stack(cmp_p), jnp.stack(cmp_s), jnp.stack(sel_p), jnp.stack(sel_s),
            jnp.stack(win_p), jnp.stack(win_s), jnp.stack(ssm_p), jnp.stack(ssm_s), jnp.stack(conv_p), jnp.stack(conv_s))
```

```python
import functools
import math

import numpy as np
import jax
import jax.numpy as jnp
from jax import lax
from jax.experimental import pallas as pl
from jax.experimental.pallas import tpu as pltpu

F32 = jnp.float32
BF16 = jnp.bfloat16
NEG = -1e30

D_MODEL = 2048
SEQ = 8192
DEPTH = 2
DEC_BATCH = 128
PAST_LEN = 2048
PAGE_SIZE = 128
N_PAGES = PAST_LEN // PAGE_SIZE
D_SSD = 4096
SSD_HEADDIM = 64
SSD_HEADS = 64
SSD_GROUPS = 8
SSD_STATE = 128
SSD_CONV = 4
SSD_CHUNK = 256
CONV_CH = 6144
NSA_HEADS = 16
NSA_KV_HEADS = 4
HEAD_DIM = 128
Q_PER_KV = 4
CMP_STRIDE = 16
SEL_BLOCK = 64
N_SEL = 16
WINDOW = 512
Q_BLOCK = 128
FORCED_SCORE = 1e4
REL_BUCKETS = 32
REL_MAX_DIST = 128
MOE_GROUPS = 4
MOE_PER_GROUP = 4
N_EXPERTS = 16
D_EXPERT = 512
RMS_EPS = 1e-6
KV_COLS = 1024
IN_SIZES = (D_SSD, CONV_CH, SSD_HEADS, NSA_HEADS * HEAD_DIM, KV_COLS, KV_COLS, KV_COLS, 3 * NSA_HEADS, 2 * D_MODEL)
SCALE = HEAD_DIM ** -0.5

C_Z = 0
C_XBC = 4096
C_Q = 10240
C_KVC = 12288
C_KVS = 13312
C_KVW = 14336
C_GM = 15360
C_SMALL = 19456
N_PROJ = 19712
GATE_LANE0 = 64

M_ALL = SEQ + DEC_BATCH
TM = 640
VMEM_LIMIT_MIB = 56


def _cp(*sem):
    return pltpu.CompilerParams(dimension_semantics=sem, vmem_limit_bytes=VMEM_LIMIT_MIB * 1024 * 1024)


def _dot(a, b):
    return jnp.dot(a, b, preferred_element_type=F32)


def _dot_nt(a, b):
    return lax.dot_general(a, b, (((1,), (1,)), ((), ())), preferred_element_type=F32)


def _dot_f32(a, b):
    return jnp.dot(a, b, preferred_element_type=F32, precision=lax.Precision.HIGHEST)


def _silu(x):
    return x * jax.nn.sigmoid(x)


def _softplus(x):
    return jnp.maximum(x, 0.0) + jnp.log1p(jnp.exp(-jnp.abs(x)))


def _split_hi_lo(v):
    hi = v.astype(BF16)
    lo = (v - hi.astype(F32)).astype(BF16)
    return hi, lo


def _expand(v, e):
    hi, lo = _split_hi_lo(v)
    return _dot(hi, e) + _dot(lo, e)


def _rmsnorm_kernel(x_ref, w_ref, o_ref):
    x = x_ref[...]
    ms = jnp.mean(x * x, axis=-1, keepdims=True)
    o_ref[...] = (x * lax.rsqrt(ms + RMS_EPS) * w_ref[...]).astype(o_ref.dtype)


def _rmsnorm(x, w, out_dtype, tm=TM):
    m, d = x.shape
    return pl.pallas_call(
        _rmsnorm_kernel, grid=(m // tm,),
        in_specs=[pl.BlockSpec((tm, d), lambda i: (i, 0)), pl.BlockSpec((1, d), lambda i: (0, 0))],
        out_specs=pl.BlockSpec((tm, d), lambda i: (i, 0)),
        out_shape=jax.ShapeDtypeStruct((m, d), out_dtype),
        compiler_params=_cp("parallel"), name="rmsnorm")(x, w.reshape(1, d))


def _gated_norm_kernel(yp_ref, ys_ref, z_ref, w_ref, o_ref, *, n_prompt_tiles):
    i = pl.program_id(0)
    y = jnp.where(i < n_prompt_tiles, yp_ref[...], ys_ref[...])
    gated = y * _silu(z_ref[...])
    ms = jnp.mean(gated * gated, axis=-1, keepdims=True)
    o_ref[...] = (gated * lax.rsqrt(ms + RMS_EPS) * w_ref[...]).astype(o_ref.dtype)


def _gated_norm(y_p, y_s, proj, w):
    tm = DEC_BATCH
    npt = SEQ // tm
    return pl.pallas_call(
        functools.partial(_gated_norm_kernel, n_prompt_tiles=npt), grid=(M_ALL // tm,),
        in_specs=[pl.BlockSpec((tm, D_SSD), lambda i: (jnp.minimum(i, npt - 1), 0)),
                  pl.BlockSpec((tm, D_SSD), lambda i: (0, 0)),
                  pl.BlockSpec((tm, D_SSD), lambda i: (i, C_Z // D_SSD)),
                  pl.BlockSpec((1, D_SSD), lambda i: (0, 0))],
        out_specs=pl.BlockSpec((tm, D_SSD), lambda i: (i, 0)),
        out_shape=jax.ShapeDtypeStruct((M_ALL, D_SSD), BF16),
        compiler_params=_cp("parallel"), name="gated_norm")(y_p, y_s, proj, w.reshape(1, D_SSD))


def _mm_kernel(a_ref, w_ref, *rest, epilogue):
    o_ref = rest[-1]
    acc = _dot(a_ref[...], w_ref[...])
    if epilogue is not None:
        acc = epilogue(acc, *[r[...] for r in rest[:-1]])
    o_ref[...] = acc.astype(o_ref.dtype)


def _mm(a, w, *, tn, out_dtype, name, extras=(), epilogue=None, tm=TM):
    m, k = a.shape
    n = w.shape[1]
    in_specs = [pl.BlockSpec((tm, k), lambda j, i: (i, 0)), pl.BlockSpec((k, tn), lambda j, i: (0, j))]
    args = [a, w]
    for arr, coff in extras:
        in_specs.append(pl.BlockSpec((tm, tn), lambda j, i, coff=coff: (i, coff + j)))
        args.append(arr)
    return pl.pallas_call(
        functools.partial(_mm_kernel, epilogue=epilogue), grid=(n // tn, m // tm),
        in_specs=in_specs, out_specs=pl.BlockSpec((tm, tn), lambda j, i: (i, j)),
        out_shape=jax.ShapeDtypeStruct((m, n), out_dtype),
        compiler_params=_cp("parallel", "parallel"), name=name)(*args)


def _merge_epilogue(y_b, y_a, g_a, g_b):
    return jax.nn.sigmoid(g_a) * y_a + jax.nn.sigmoid(g_b) * y_b


def _resid_epilogue(acc, x):
    return x + acc


CONV_TR = 512
CONV_TC = 512


def _conv_prompt_kernel(x_ref, halo_ref, w_ref, b_ref, o_ref):
    i = pl.program_id(1)
    halo = jnp.where(i > 0, halo_ref[...], 0.0)
    x = jnp.concatenate([halo, x_ref[...]], axis=0)
    n = x.shape[0]
    acc = b_ref[...] + x[8:] * w_ref[SSD_CONV - 1:SSD_CONV, :]
    for k in range(SSD_CONV - 1):
        shifted = pltpu.roll(x, SSD_CONV - 1 - k, axis=0)
        acc = acc + shifted[8:] * w_ref[k:k + 1, :]
    del n
    o_ref[...] = _silu(acc)


def _conv_prompt(proj, conv_w, conv_b):
    nct = CONV_CH // CONV_TC
    nrt = SEQ // CONV_TR
    c0 = C_XBC // CONV_TC
    return pl.pallas_call(
        _conv_prompt_kernel, grid=(nct, nrt),
        in_specs=[pl.BlockSpec((CONV_TR, CONV_TC), lambda c, i: (i, c0 + c)),
                  pl.BlockSpec((8, CONV_TC), lambda c, i: (jnp.maximum(i * (CONV_TR // 8) - 1, 0), c0 + c)),
                  pl.BlockSpec((SSD_CONV, CONV_TC), lambda c, i: (0, c)),
                  pl.BlockSpec((1, CONV_TC), lambda c, i: (0, c))],
        out_specs=pl.BlockSpec((CONV_TR, CONV_TC), lambda c, i: (i, c)),
        out_shape=jax.ShapeDtypeStruct((SEQ, CONV_CH), F32),
        compiler_params=_cp("parallel", "parallel"), name="conv_prompt")(
            proj, proj, conv_w, conv_b.reshape(1, CONV_CH))


GW = SSD_HEADS // SSD_GROUPS * SSD_HEADDIM
HPG = SSD_HEADS // SSD_GROUPS


def _scan_kernel(xs_ref, b_ref, c_ref, bt_ref, dtr_ref, dtrt_ref, dtb_r_ref, dtb_c_ref, al_r_ref, al_c_ref,
                 dskip_ref, e_ref, y_ref, hout_ref, h_scr):
    c = pl.program_id(0)
    g = pl.program_id(1)
    q = SSD_CHUNK

    @pl.when(c == 0)
    def _():
        h_scr[g] = jnp.zeros((SSD_STATE, GW), F32)

    dt = _softplus(dtr_ref[...] + dtb_r_ref[...])
    d_a = dt * (-jnp.exp(al_r_ref[...]))
    dt_t = _softplus(dtrt_ref[...] + dtb_c_ref[...])
    d_a_t = dt_t * (-jnp.exp(al_c_ref[...]))
    row = lax.broadcasted_iota(jnp.int32, (q, q), 0)
    col = lax.broadcasted_iota(jnp.int32, (q, q), 1)
    tril = row >= col
    cum = _dot_f32(tril.astype(F32), d_a)
    cum_t = _dot_f32(d_a_t, (col >= row).astype(F32))
    cum_last = cum[q - 1:q, :]
    e = e_ref[...]
    dt_e = _expand(dt, e)
    ecum_e = _expand(jnp.exp(cum), e)
    tail_e = _expand(jnp.exp(cum_last - cum), e)
    elast_e = _expand(jnp.broadcast_to(jnp.exp(cum_last), (8, 128)), e)[0:1, :]

    xs = xs_ref[...]
    xdt = xs * dt_e
    xdt_b = xdt.astype(BF16)
    xw = (xdt * tail_e).astype(BF16)
    bg = b_ref[...].astype(BF16)
    cg = c_ref[...].astype(BF16)
    cb = _dot_nt(cg, bg)
    h_prev = h_scr[g]
    y = _dot(cg, h_prev.astype(BF16)) * ecum_e
    lane = lax.broadcasted_iota(jnp.int32, (q, 128), 1)
    ys = []
    for pair in range(HPG // 2):
        xpair = xdt_b[:, pair * 128:(pair + 1) * 128]
        top = jnp.where(lane < SSD_HEADDIM, xpair, jnp.zeros_like(xpair))
        bot = jnp.where(lane >= SSD_HEADDIM, xpair, jnp.zeros_like(xpair))
        atts = []
        for hh in range(2):
            h8 = pair * 2 + hh
            seg = jnp.exp(jnp.where(tril, cum[:, h8:h8 + 1] - cum_t[h8:h8 + 1, :], NEG))
            atts.append((cb * seg).astype(BF16))
        ys.append(_dot(jnp.concatenate(atts, axis=1), jnp.concatenate([top, bot], axis=0)))
    y = y + jnp.concatenate(ys, axis=1) + dskip_ref[...] * xs
    y_ref[...] = y
    h_new = h_prev * elast_e + _dot(bt_ref[...], xw)
    h_scr[g] = h_new
    hout_ref[...] = h_new


def _head_expand_matrix(n_heads, rows):
    e = np.zeros((rows, n_heads * SSD_HEADDIM), np.float32)
    for h in range(n_heads):
        e[h, h * SSD_HEADDIM:(h + 1) * SSD_HEADDIM] = 1.0
    return jnp.asarray(e, BF16)


def _ssd_scan_prompt(act, bt, dtr, dt_bias, a_log, d_skip):
    q = SSD_CHUNK
    nc = SEQ // q
    dtr_g = dtr.reshape(SEQ, SSD_GROUPS, HPG).transpose(1, 0, 2)
    dtr_g128 = jnp.pad(dtr_g, ((0, 0), (0, 0), (0, 128 - HPG)))
    dtrt_g = dtr_g.transpose(0, 2, 1)
    pad_r = lambda v: jnp.pad(v.reshape(SSD_GROUPS, 1, HPG), ((0, 0), (0, 0), (0, 128 - HPG)))
    col_c = lambda v: v.reshape(SSD_GROUPS, HPG, 1)
    dskip_e = jnp.repeat(d_skip, SSD_HEADDIM).reshape(1, D_SSD)
    e = _head_expand_matrix(HPG, 128)
    small = lambda shape: pl.BlockSpec((None,) + shape, lambda c, g: (g, 0, 0))
    y, h_out = pl.pallas_call(
        _scan_kernel, grid=(nc, SSD_GROUPS),
        in_specs=[pl.BlockSpec((q, GW), lambda c, g: (c, g)),
                  pl.BlockSpec((q, SSD_STATE), lambda c, g: (c, D_SSD // SSD_STATE + g)),
                  pl.BlockSpec((q, SSD_STATE), lambda c, g: (c, (D_SSD + SSD_GROUPS * SSD_STATE) // SSD_STATE + g)),
                  pl.BlockSpec((SSD_STATE, q), lambda c, g: (g, c)),
                  pl.BlockSpec((None, q, 128), lambda c, g: (g, c, 0)),
                  pl.BlockSpec((None, HPG, q), lambda c, g: (g, 0, c)),
                  small((1, 128)), small((HPG, 1)), small((1, 128)), small((HPG, 1)),
                  pl.BlockSpec((1, GW), lambda c, g: (0, g)),
                  pl.BlockSpec((128, GW), lambda c, g: (0, 0))],
        out_specs=[pl.BlockSpec((q, GW), lambda c, g: (c, g)),
                   pl.BlockSpec((None, SSD_STATE, GW), lambda c, g: (g, 0, 0))],
        out_shape=[jax.ShapeDtypeStruct((SEQ, D_SSD), F32),
                   jax.ShapeDtypeStruct((SSD_GROUPS, SSD_STATE, GW), F32)],
        scratch_shapes=[pltpu.VMEM((SSD_GROUPS, SSD_STATE, GW), F32)],
        compiler_params=_cp("arbitrary", "arbitrary"), name="ssd_scan")(
            act, act, act, bt, dtr_g128, dtrt_g, pad_r(dt_bias), col_c(dt_bias), pad_r(a_log), col_c(a_log),
            dskip_e, e)
    h_last = h_out.reshape(SSD_GROUPS, SSD_STATE, HPG, SSD_HEADDIM).transpose(0, 2, 3, 1)
    return y, h_last.reshape(SSD_HEADS, SSD_HEADDIM, SSD_STATE)


def _conv_sample_kernel(s_ref, x_ref, w_ref, b_ref, o_ref):
    acc = b_ref[...] + x_ref[...] * w_ref[SSD_CONV - 1:SSD_CONV, :]
    for k in range(SSD_CONV - 1):
        acc = acc + s_ref[k] * w_ref[k:k + 1, :]
    o_ref[...] = _silu(acc)


def _conv_sample(state_t, xbc, conv_w, conv_b):
    tc = 1024
    return pl.pallas_call(
        _conv_sample_kernel, grid=(CONV_CH // tc,),
        in_specs=[pl.BlockSpec((SSD_CONV - 1, DEC_BATCH, tc), lambda c: (0, 0, c)),
                  pl.BlockSpec((DEC_BATCH, tc), lambda c: (0, c)),
                  pl.BlockSpec((SSD_CONV, tc), lambda c: (0, c)),
                  pl.BlockSpec((1, tc), lambda c: (0, c))],
        out_specs=pl.BlockSpec((DEC_BATCH, tc), lambda c: (0, c)),
        out_shape=jax.ShapeDtypeStruct((DEC_BATCH, CONV_CH), F32),
        compiler_params=_cp("parallel"), name="conv_sample")(state_t, xbc, conv_w, conv_b.reshape(1, CONV_CH))


def _ssd_step_kernel(h_ref, xs_ref, dtr_ref, bn_ref, cn_ref, dtb_ref, al_ref, dskip_ref, e64_ref, e8_ref,
                     y_ref, hout_ref):
    dt = _softplus(dtr_ref[...] + dtb_ref[...])
    dec = jnp.exp(dt * (-jnp.exp(al_ref[...])))
    e64 = e64_ref[...]
    dt_e = _expand(dt, e64)[0:1, :]
    dec_e = _expand(dec, e64)[0:1, :]
    xs = xs_ref[...]
    xdt = xs * dt_e
    e8 = e8_ref[...]
    b_exp = _dot(bn_ref[...].astype(BF16), e8)
    c_exp = _dot(cn_ref[...].astype(BF16), e8)
    h_new = h_ref[...] * dec_e + b_exp * xdt
    hout_ref[...] = h_new
    y_ref[...] = jnp.sum(h_new * c_exp, axis=0, keepdims=True) + dskip_ref[...] * xs


def _ssd_step_sample(state_t, act_s, dtr_s, dt_bias, a_log, d_skip):
    b = DEC_BATCH
    xs3 = act_s[:, :D_SSD].reshape(b, 1, D_SSD)
    bn = act_s[:, D_SSD:D_SSD + 1024].reshape(b, SSD_GROUPS, SSD_STATE).transpose(0, 2, 1)
    cn = act_s[:, D_SSD + 1024:].reshape(b, SSD_GROUPS, SSD_STATE).transpose(0, 2, 1)
    bn = jnp.pad(bn, ((0, 0), (0, 0), (0, 128 - SSD_GROUPS)))
    cn = jnp.pad(cn, ((0, 0), (0, 0), (0, 128 - SSD_GROUPS)))
    dtr8 = jnp.pad(dtr_s.reshape(b, 1, SSD_HEADS), ((0, 0), (0, 7), (0, 128 - SSD_HEADS)))
    row128 = lambda v: jnp.pad(v.reshape(1, SSD_HEADS), ((0, 0), (0, 128 - SSD_HEADS)))
    dskip_e = jnp.repeat(d_skip, SSD_HEADDIM).reshape(1, D_SSD)
    e64 = _head_expand_matrix(SSD_HEADS, 128)
    e8np = np.zeros((128, D_SSD), np.float32)
    for g in range(SSD_GROUPS):
        e8np[g, g * GW:(g + 1) * GW] = 1.0
    e8 = jnp.asarray(e8np, BF16)
    full = lambda shape: pl.BlockSpec(shape, lambda i: (0,) * len(shape))
    per_b = lambda shape: pl.BlockSpec((None,) + shape, lambda i: (i, 0, 0))
    return pl.pallas_call(
        _ssd_step_kernel, grid=(b,),
        in_specs=[per_b((SSD_STATE, D_SSD)), per_b((1, D_SSD)), per_b((8, 128)),
                  per_b((SSD_STATE, 128)), per_b((SSD_STATE, 128)),
                  full((1, 128)), full((1, 128)), full((1, D_SSD)), full((128, D_SSD)), full((128, D_SSD))],
        out_specs=[per_b((1, D_SSD)), per_b((SSD_STATE, D_SSD))],
        out_shape=[jax.ShapeDtypeStruct((b, 1, D_SSD), F32), jax.ShapeDtypeStruct((b, SSD_STATE, D_SSD), F32)],
        compiler_params=_cp("parallel"), name="ssd_step")(
            state_t, xs3, dtr8, bn, cn, row128(dt_bias), row128(a_log), dskip_e, e64, e8)


def _chunk_proj_kernel(pt_ref, *refs):
    del pt_ref
    pages = refs[:N_PAGES]
    w_ref, o_ref, x_scr = refs[N_PAGES:]
    for hs in range(2 * NSA_KV_HEADS):
        for pi, pg in enumerate(pages):
            x_scr[hs, pi * PAGE_SIZE:(pi + 1) * PAGE_SIZE, :] = pg[:, hs * HEAD_DIM:(hs + 1) * HEAD_DIM]
    nch = N_PAGES * PAGE_SIZE // CMP_STRIDE
    for s in range(2):
        lhs_h = []
        for h in range(NSA_KV_HEADS):
            per_r = [x_scr[h * 2 + s, pl.ds(r, nch, stride=CMP_STRIDE), :].astype(BF16)
                     for r in range(CMP_STRIDE)]
            lhs_h.append(jnp.concatenate(per_r, axis=1))
        res = _dot(jnp.concatenate(lhs_h, axis=0), w_ref[s])
        for h in range(NSA_KV_HEADS):
            o_ref[s, h] = res[h * nch:(h + 1) * nch]


def _chunk_proj(pages_arr, page_spec_fn, page_table, w1c, n_groups):
    nch = N_PAGES * PAGE_SIZE // CMP_STRIDE
    in_specs = [page_spec_fn(p) for p in range(N_PAGES)]
    in_specs.append(pl.BlockSpec((2, CMP_STRIDE * HEAD_DIM, 2 * HEAD_DIM), lambda b, pt: (0, 0, 0)))
    gs = pltpu.PrefetchScalarGridSpec(
        num_scalar_prefetch=1, grid=(n_groups,), in_specs=in_specs,
        out_specs=pl.BlockSpec((2, NSA_KV_HEADS, nch, 2 * HEAD_DIM), lambda b, pt: (0, 0, b, 0)),
        scratch_shapes=[pltpu.VMEM((2 * NSA_KV_HEADS, N_PAGES * PAGE_SIZE, HEAD_DIM), F32)])
    return pl.pallas_call(
        _chunk_proj_kernel, grid_spec=gs,
        out_shape=jax.ShapeDtypeStruct((2, NSA_KV_HEADS, n_groups * nch, 2 * HEAD_DIM), F32),
        compiler_params=_cp("parallel"), name="cmp_chunk_proj")(page_table, *([pages_arr] * N_PAGES), w1c)


def _cmp_bias_kernel(pe_ref, w_ref, b_ref, o_ref):
    o_ref[...] = b_ref[...] + _dot(pe_ref[...], w_ref[...])


def _cmp_bias(cmp_pe, cmp_w1, cmp_b1):
    k = 2 * CMP_STRIDE * HEAD_DIM
    pe = jnp.broadcast_to(cmp_pe.reshape(2, 1, k), (2, 8, k)).astype(BF16)
    w = cmp_w1.reshape(2, k, HEAD_DIM).astype(BF16)
    b = jnp.broadcast_to(cmp_b1.reshape(2, 1, HEAD_DIM), (2, 8, HEAD_DIM))
    return pl.pallas_call(
        _cmp_bias_kernel, grid=(2,),
        in_specs=[pl.BlockSpec((None, 8, k), lambda s: (s, 0, 0)),
                  pl.BlockSpec((None, k, HEAD_DIM), lambda s: (s, 0, 0)),
                  pl.BlockSpec((None, 8, HEAD_DIM), lambda s: (s, 0, 0))],
        out_specs=pl.BlockSpec((None, 8, HEAD_DIM), lambda s: (s, 0, 0)),
        out_shape=jax.ShapeDtypeStruct((2, 8, HEAD_DIM), F32),
        compiler_params=_cp("parallel"), name="cmp_bias")(pe, w, b)


def _cmp_mlp_kernel(p_ref, b_ref, w2_ref, o_ref):
    p = p_ref[...]
    n = p.shape[0]
    nxt = pltpu.roll(p[:, HEAD_DIM:], n - 1, axis=0)
    hid = _silu(p[:, :HEAD_DIM] + nxt + b_ref[0:1, :])
    o_ref[...] = _dot(hid.astype(BF16), w2_ref[...]).astype(o_ref.dtype)


def _cmp_mlp(p, b1c, w2, n_groups, out_dtype):
    nch = p.shape[2] // n_groups
    return pl.pallas_call(
        _cmp_mlp_kernel, grid=(2, NSA_KV_HEADS, n_groups),
        in_specs=[pl.BlockSpec((None, None, nch, 2 * HEAD_DIM), lambda s, h, b: (s, h, b, 0)),
                  pl.BlockSpec((None, 8, HEAD_DIM), lambda s, h, b: (s, 0, 0)),
                  pl.BlockSpec((None, HEAD_DIM, HEAD_DIM), lambda s, h, b: (s, 0, 0))],
        out_specs=pl.BlockSpec((None, None, nch, HEAD_DIM), lambda s, h, b: (b, h * 2 + s, 0, 0)),
        out_shape=jax.ShapeDtypeStruct((n_groups, 2 * NSA_KV_HEADS, nch, HEAD_DIM), out_dtype),
        compiler_params=_cp("parallel", "parallel", "parallel"), name="cmp_mlp")(p, b1c, w2)


def _bucket_np(dist):
    n = np.maximum(dist, 0)
    exact = REL_BUCKETS // 2
    nf = np.maximum(n, 1).astype(np.float32)
    big = exact + (np.log(nf / np.float32(exact)) / np.float32(math.log(REL_MAX_DIST / exact))
                   * np.float32(REL_BUCKETS - exact)).astype(np.int32)
    return np.where(n < exact, n, np.minimum(big, REL_BUCKETS - 1))


def _bias_table(rel_bias_t, dist, valid):
    tab = rel_bias_t[:, _bucket_np(dist)]
    return jnp.where(jnp.asarray(valid), tab, NEG)


CMP_PAD = 120
CMP_ROWS = 768


def _wsel_np(n_cmp, n_blocks, rows, pad):
    w = np.zeros((rows, 128), np.float32)
    c0 = np.arange(n_cmp)[:, None] * CMP_STRIDE
    s0 = np.arange(n_blocks)[None, :] * SEL_BLOCK
    inter = np.minimum(c0 + 2 * CMP_STRIDE, s0 + SEL_BLOCK) - np.maximum(c0, s0)
    w[pad:pad + n_cmp, :n_blocks] = np.maximum(inter, 0).astype(np.float32) / (2 * CMP_STRIDE)
    return w


def _topk_mask(score, n_pick):
    lane = lax.broadcasted_iota(jnp.int32, score.shape, 1).astype(F32)
    sel = jnp.zeros(score.shape, F32)
    for _ in range(n_pick):
        mx = jnp.max(score, axis=-1, keepdims=True)
        idx = jnp.min(jnp.where(score == mx, lane, 1e9), axis=-1, keepdims=True)
        hit = lane == idx
        sel = jnp.where(hit, 1.0, sel)
        score = jnp.where(hit, -3e38, score)
    return sel


def _nsa_prompt_kernel(cb_ref, q_ref, kc_ref, vc_ref, ks_ref, vs_ref, kw_ref, vw_ref, small_ref,
                       tc_ref, td_ref, tp_ref, wsel_ref, o_ref, m_scr, l_scr, acc_scr, out_scr, sel_scr):
    g = pl.program_id(0)
    i = pl.program_id(1)
    qb = Q_BLOCK
    qf = q_ref[...]
    qs = [qf[:, j * HEAD_DIM:(j + 1) * HEAD_DIM].astype(BF16) for j in range(Q_PER_KV)]
    q_all = jnp.concatenate(qs, axis=0)
    row = lax.broadcasted_iota(jnp.int32, (qb, qb), 0)
    col = lax.broadcasted_iota(jnp.int32, (qb, qb), 1)
    cbs = [cb_ref[g * Q_PER_KV + j] for j in range(Q_PER_KV)]

    sig = jax.nn.sigmoid(small_ref[...])
    lane_s = lax.broadcasted_iota(jnp.int32, sig.shape, 1)

    def gate(branch, j):
        cidx = GATE_LANE0 + branch * NSA_HEADS + g * Q_PER_KV + j
        return jnp.sum(jnp.where(lane_s == cidx, sig, 0.0), axis=-1, keepdims=True)

    n_old = 512
    d0 = pl.multiple_of(i * (qb // CMP_STRIDE), 8)
    kc_old = kc_ref[0:n_old, :].astype(BF16)
    vc_old = vc_ref[0:n_old, :].astype(BF16)
    kc_d = kc_ref[pl.ds(d0, qb), :].astype(BF16)
    vc_d = vc_ref[pl.ds(d0, qb), :].astype(BF16)
    ncol = lax.broadcasted_iota(jnp.int32, (qb, n_old), 1)
    old_valid = (ncol >= CMP_PAD) & (ncol < i * (qb // CMP_STRIDE))
    pad_valid = (col + i * (qb // CMP_STRIDE)) >= CMP_PAD
    psum_old = jnp.zeros((qb, n_old), F32)
    psum_d = jnp.zeros((qb, qb), F32)
    for j in range(Q_PER_KV):
        tcj = tc_ref[j]
        d_valid = (tcj > 0.5 * NEG) & pad_valid
        s_o = jnp.where(old_valid, _dot_nt(qs[j], kc_old) * SCALE + cbs[j], NEG)
        s_d = jnp.where(d_valid, _dot_nt(qs[j], kc_d) * SCALE + tcj, NEG)
        mx = jnp.maximum(jnp.max(s_o, axis=-1, keepdims=True), jnp.max(s_d, axis=-1, keepdims=True))
        e_o = jnp.where(old_valid, jnp.exp(s_o - mx), 0.0)
        e_d = jnp.where(d_valid, jnp.exp(s_d - mx), 0.0)
        den = jnp.maximum(jnp.sum(e_o, axis=-1, keepdims=True) + jnp.sum(e_d, axis=-1, keepdims=True), 1e-30)
        p_o = e_o / den
        p_d = e_d / den
        o_c = _dot(p_o.astype(BF16), vc_old) + _dot(p_d.astype(BF16), vc_d)
        out_scr[j] = gate(0, j) * o_c
        psum_old = psum_old + p_o
        psum_d = psum_d + p_d
    imp = (_dot(psum_old.astype(BF16), wsel_ref[0:n_old, :].astype(BF16))
           + _dot(psum_d.astype(BF16), wsel_ref[pl.ds(d0, qb), :].astype(BF16)))

    cur = 2 * i + (row >= SEL_BLOCK).astype(jnp.int32)
    forced = (col == 0) | (col == cur) | (col == cur - 1)
    score = jnp.where(forced, FORCED_SCORE, jnp.where(col <= cur, imp, -1.0))
    sel_scr[...] = _topk_mask(score, N_SEL).astype(BF16)

    def reset():
        m_scr[...] = jnp.full(m_scr.shape, NEG, F32)
        l_scr[...] = jnp.zeros(l_scr.shape, F32)
        acc_scr[...] = jnp.zeros(acc_scr.shape, F32)

    def flash_tile(k, v, bias_fn, mask):
        s_all = _dot_nt(q_all, k) * SCALE
        ps = []
        for j in range(Q_PER_KV):
            bias = bias_fn(j)
            s = s_all[j * qb:(j + 1) * qb] + bias
            valid = mask
            if not isinstance(bias, jax.Array) or bias.ndim == 0:
                pass
            else:
                tv = bias > 0.5 * NEG
                valid = tv if valid is None else (valid & tv)
            if valid is not None:
                s = jnp.where(valid, s, NEG)
            m_prev = m_scr[j]
            m_new = jnp.maximum(m_prev, jnp.max(s, axis=-1, keepdims=True))
            p = jnp.exp(s - m_new)
            if valid is not None:
                p = jnp.where(valid, p, 0.0)
            alpha = jnp.exp(m_prev - m_new)
            l_scr[j] = alpha * l_scr[j] + jnp.sum(p, axis=-1, keepdims=True)
            m_scr[j] = m_new
            acc_scr[j] = acc_scr[j] * alpha
            ps.append(p.astype(BF16))
        pv = _dot(jnp.concatenate(ps, axis=0), v)
        for j in range(Q_PER_KV):
            acc_scr[j] = acc_scr[j] + pv[j * qb:(j + 1) * qb]

    def finish(branch):
        for j in range(Q_PER_KV):
            o = acc_scr[j] / jnp.maximum(l_scr[j], 1e-30)
            out_scr[j] = out_scr[j] + gate(branch, j) * o

    def tile(ref, kt):
        return ref[pl.ds(pl.multiple_of(kt * qb, qb), qb), :]

    def sel_mask(kt):
        expand = (row == 2 * kt + (col >= SEL_BLOCK).astype(jnp.int32)).astype(BF16)
        return _dot(sel_scr[...], expand) > 0.5

    reset()

    def sel_body(kt, carry):
        flash_tile(tile(ks_ref, kt), tile(vs_ref, kt), lambda j: cbs[j], sel_mask(kt))
        return carry

    lax.fori_loop(0, jnp.maximum(i - 1, 0), sel_body, 0)

    @pl.when(i >= 1)
    def _():
        flash_tile(tile(ks_ref, i - 1), tile(vs_ref, i - 1), lambda j: tp_ref[j], sel_mask(i - 1))

    flash_tile(tile(ks_ref, i), tile(vs_ref, i), lambda j: td_ref[j], sel_mask(i))
    finish(1)

    reset()
    for back in range(WINDOW // qb, -1, -1):
        @pl.when(i >= back)
        def _(back=back):
            kt = i - back
            if back == WINDOW // qb:
                flash_tile(tile(kw_ref, kt), tile(vw_ref, kt), lambda j: cbs[j], col > row)
            elif back >= 2:
                flash_tile(tile(kw_ref, kt), tile(vw_ref, kt), lambda j: cbs[j], None)
            elif back == 1:
                flash_tile(tile(kw_ref, kt), tile(vw_ref, kt), lambda j: tp_ref[j], None)
            else:
                flash_tile(tile(kw_ref, kt), tile(vw_ref, kt), lambda j: td_ref[j], None)
    finish(2)

    for j in range(Q_PER_KV):
        o_ref[:, j * HEAD_DIM:(j + 1) * HEAD_DIM] = out_scr[j].astype(o_ref.dtype)


def _nsa_prompt(proj, kcv, kvs_b, kvw_b, rel_bias, seq):
    qb = Q_BLOCK
    nb = seq // qb
    rbt = rel_bias.T
    a = np.arange(qb)[:, None]
    b = np.arange(qb)[None, :]
    td = _bias_table(rbt, a - b, a >= b)
    tp = _bias_table(rbt, a - b + qb, np.ones((qb, qb), bool))
    dist_c = a - CMP_STRIDE * b + (CMP_STRIDE * CMP_PAD - 2 * CMP_STRIDE + 1)
    tc = _bias_table(rbt, dist_c, dist_c >= 0)
    cb = rel_bias[REL_BUCKETS - 1]
    n_cmp = seq // CMP_STRIDE - 1
    wsel = jnp.asarray(_wsel_np(n_cmp, seq // SEL_BLOCK, CMP_ROWS, CMP_PAD), F32)
    tab = lambda: pl.BlockSpec((Q_PER_KV, qb, qb), lambda g, i: (g, 0, 0))
    kv = lambda arr_cols: pl.BlockSpec((seq, HEAD_DIM), arr_cols)
    return pl.pallas_call(
        _nsa_prompt_kernel, grid=(NSA_KV_HEADS, nb),
        in_specs=[pl.BlockSpec(memory_space=pltpu.SMEM),
                  pl.BlockSpec((qb, Q_PER_KV * HEAD_DIM), lambda g, i: (i, C_Q // (Q_PER_KV * HEAD_DIM) + g)),
                  pl.BlockSpec((None, CMP_ROWS, HEAD_DIM), lambda g, i: (2 * g, 0, 0)),
                  pl.BlockSpec((None, CMP_ROWS, HEAD_DIM), lambda g, i: (2 * g + 1, 0, 0)),
                  kv(lambda g, i: (0, 2 * g)), kv(lambda g, i: (0, 2 * g + 1)),
                  kv(lambda g, i: (0, 2 * g)), kv(lambda g, i: (0, 2 * g + 1)),
                  pl.BlockSpec((qb, 256), lambda g, i: (i, C_SMALL // 256)),
                  tab(), tab(), tab(),
                  pl.BlockSpec((CMP_ROWS, 128), lambda g, i: (0, 0))],
        out_specs=pl.BlockSpec((qb, Q_PER_KV * HEAD_DIM), lambda g, i: (i, g)),
        out_shape=jax.ShapeDtypeStruct((seq, NSA_HEADS * HEAD_DIM), BF16),
        scratch_shapes=[pltpu.VMEM((Q_PER_KV, qb, 1), F32), pltpu.VMEM((Q_PER_KV, qb, 1), F32),
                        pltpu.VMEM((Q_PER_KV, qb, HEAD_DIM), F32), pltpu.VMEM((Q_PER_KV, qb, HEAD_DIM), F32),
                        pltpu.VMEM((qb, 128), BF16)],
        compiler_params=_cp("parallel", "arbitrary"), name="nsa_prompt")(
            cb, proj, kcv, kcv, kvs_b, kvs_b, kvw_b, kvw_b, proj, tc, td, tp, wsel)


N_BLK_S = -(-(PAST_LEN + 1) // SEL_BLOCK)
N_OLD_SEL = N_SEL - 1


def _nsa_sample_cmp_kernel(q_ref, kcv_ref, bias_ref, wsel_ref, oc_ref, idx_ref):
    q = q_ref[...].astype(BF16)
    hrow = lax.broadcasted_iota(jnp.int32, (NSA_HEADS, 128), 0) // Q_PER_KV
    o_c = jnp.zeros((NSA_HEADS, HEAD_DIM), F32)
    psums = []
    bias = bias_ref[...]
    valid = bias > 0.5 * NEG
    for g in range(NSA_KV_HEADS):
        s = jnp.where(valid, _dot_nt(q, kcv_ref[2 * g]) * SCALE + bias, NEG)
        mx = jnp.max(s, axis=-1, keepdims=True)
        e = jnp.where(valid, jnp.exp(s - mx), 0.0)
        p = e / jnp.maximum(jnp.sum(e, axis=-1, keepdims=True), 1e-30)
        mine = hrow == g
        o_c = o_c + jnp.where(mine, _dot(p.astype(BF16), kcv_ref[2 * g + 1]), 0.0)
        psums.append(jnp.sum(jnp.where(mine, p, 0.0), axis=0, keepdims=True))
    oc_ref[...] = o_c
    psum = jnp.concatenate(psums + [jnp.zeros((8 - NSA_KV_HEADS, 128), F32)], axis=0)
    imp = _dot(psum.astype(BF16), wsel_ref[...])
    lane = lax.broadcasted_iota(jnp.int32, imp.shape, 1)
    cur = N_BLK_S - 1
    forced = (lane == 0) | (lane == cur - 1)
    score = jnp.where(lane >= cur, -3e38, jnp.where(forced, FORCED_SCORE, imp))
    lanef = lane.astype(F32)
    out = jnp.zeros(imp.shape, F32)
    for r in range(N_OLD_SEL):
        mx = jnp.max(score, axis=-1, keepdims=True)
        idx = jnp.min(jnp.where(score == mx, lanef, 1e9), axis=-1, keepdims=True)
        out = jnp.where(lane == r, idx, out)
        score = jnp.where(lanef == idx, -3e38, score)
    idx_ref[...] = out.astype(jnp.int32)


def _nsa_sample_cmp(q_s, kcv_s, rel_bias):
    b = DEC_BATCH
    rbt = rel_bias.T
    n = np.arange(128)[None, :]
    dist = PAST_LEN - (CMP_STRIDE * n + 2 * CMP_STRIDE - 1)
    bias = _bias_table(rbt, dist, n < (PAST_LEN // CMP_STRIDE - 1))[:, 0, :]
    wsel = jnp.asarray(_wsel_np(PAST_LEN // CMP_STRIDE - 1, N_BLK_S, 128, 0), BF16)
    return pl.pallas_call(
        _nsa_sample_cmp_kernel, grid=(b,),
        in_specs=[pl.BlockSpec((None, NSA_HEADS, HEAD_DIM), lambda i: (i, 0, 0)),
                  pl.BlockSpec((None, 2 * NSA_KV_HEADS, 128, HEAD_DIM), lambda i: (i, 0, 0, 0)),
                  pl.BlockSpec((NSA_HEADS, 128), lambda i: (0, 0)),
                  pl.BlockSpec((128, 128), lambda i: (0, 0))],
        out_specs=[pl.BlockSpec((None, NSA_HEADS, HEAD_DIM), lambda i: (i, 0, 0)),
                   pl.BlockSpec((None, 8, 128), lambda i: (i, 0, 0))],
        out_shape=[jax.ShapeDtypeStruct((b, NSA_HEADS, HEAD_DIM), F32),
                   jax.ShapeDtypeStruct((b, 8, 128), jnp.int32)],
        compiler_params=_cp("parallel"), name="nsa_sample_cmp")(q_s, kcv_s, bias, wsel)


def _nsa_sample_attn_kernel(pt_ref, idx_ref, *refs):
    del pt_ref
    blocks = refs[:N_OLD_SEL]
    (win_ref, news_ref, neww_ref, q_ref, oc_ref, gn_ref, bs_ref, bw_ref, b0_ref, o_ref) = refs[N_OLD_SEL:]
    b = pl.program_id(0)
    g = pl.program_id(1)
    q = q_ref[...].astype(BF16)
    gates = jax.nn.sigmoid(gn_ref[...])
    b0 = b0_ref[...]

    def attend(pieces, new_ref):
        new = new_ref[...]
        s_new = jnp.sum(q.astype(F32) * new[:, :HEAD_DIM].astype(BF16).astype(F32), axis=-1, keepdims=True)
        s_new = s_new * SCALE + b0
        mx = s_new
        for s, valid, _ in pieces:
            mx = jnp.maximum(mx, jnp.max(s, axis=-1, keepdims=True))
        e_new = jnp.exp(s_new - mx)
        den = e_new
        acc = e_new.astype(BF16).astype(F32) * new[:, HEAD_DIM:].astype(BF16).astype(F32)
        for s, valid, v in pieces:
            e = jnp.exp(s - mx)
            if valid is not None:
                e = jnp.where(valid, e, 0.0)
            den = den + jnp.sum(e, axis=-1, keepdims=True)
            acc = acc + _dot(e.astype(BF16), v)
        return acc / jnp.maximum(den, 1e-30)

    pieces = []
    for kk in range(N_OLD_SEL):
        blk = blocks[kk][...]
        m = idx_ref[(b * NSA_KV_HEADS + g) * N_OLD_SEL + kk]
        bias = bs_ref[m]
        s = _dot_nt(q, blk[:, :HEAD_DIM].astype(BF16)) * SCALE + bias
        pieces.append((s, None, blk[:, HEAD_DIM:].astype(BF16)))
    o_s = attend(pieces, news_ref)

    win = win_ref[...]
    bw = bw_ref[...]
    valid_w = bw > 0.5 * NEG
    s_w = jnp.where(valid_w, _dot_nt(q, win[:, :HEAD_DIM].astype(BF16)) * SCALE + bw, NEG)
    o_w = attend([(s_w, valid_w, win[:, HEAD_DIM:].astype(BF16))], neww_ref)

    out = gates[:, 0:1] * oc_ref[...] + gates[:, 1:2] * o_s + gates[:, 2:3] * o_w
    o_ref[...] = out.astype(o_ref.dtype)


def _nsa_sample_attn(layer, cache_sel, cache_win, page_table, idx, kvs_new, kvw_new, q_s, o_c, gn_s, rel_bias):
    b = DEC_BATCH
    hk = NSA_KV_HEADS
    rbt = rel_bias.T
    m = np.arange(N_BLK_S - 1)[:, None]
    r = np.arange(SEL_BLOCK)[None, :]
    bs = _bias_table(rbt, PAST_LEN - (SEL_BLOCK * m + r), np.ones((N_BLK_S - 1, SEL_BLOCK), bool))
    bs = bs.reshape(hk, Q_PER_KV, N_BLK_S - 1, SEL_BLOCK).transpose(0, 2, 1, 3)
    rw = np.arange(WINDOW)[None, :]
    bw = _bias_table(rbt, WINDOW - rw, rw >= 1)[:, 0, :].reshape(hk, Q_PER_KV, WINDOW)
    b0 = rel_bias[0].reshape(hk, Q_PER_KV, 1)
    csel = cache_sel.reshape(DEPTH, -1, PAGE_SIZE // SEL_BLOCK, SEL_BLOCK, KV_COLS)
    cwin = cache_win.reshape(DEPTH, b, WINDOW, KV_COLS)
    per_page = PAGE_SIZE // SEL_BLOCK

    def sel_spec(kk):
        def imap(bi, g, pt, ix):
            blk = ix[(bi * hk + g) * N_OLD_SEL + kk]
            return (layer, pt[bi * N_PAGES + blk // per_page], blk % per_page, 0, g)
        return pl.BlockSpec((None, None, None, SEL_BLOCK, 2 * HEAD_DIM), imap)

    bgj = lambda last: pl.BlockSpec((None, None, Q_PER_KV, last), lambda bi, g, pt, ix: (bi, g, 0, 0))
    new_spec = pl.BlockSpec((None, None, 1, 2 * HEAD_DIM), lambda bi, g, pt, ix: (bi, g, 0, 0))
    in_specs = [sel_spec(kk) for kk in range(N_OLD_SEL)] + [
        pl.BlockSpec((None, None, WINDOW, 2 * HEAD_DIM), lambda bi, g, pt, ix: (layer, bi, 0, g)),
        new_spec, new_spec, bgj(HEAD_DIM), bgj(HEAD_DIM), bgj(3),
        pl.BlockSpec((None, N_BLK_S - 1, Q_PER_KV, SEL_BLOCK), lambda bi, g, pt, ix: (g, 0, 0, 0)),
        pl.BlockSpec((None, Q_PER_KV, WINDOW), lambda bi, g, pt, ix: (g, 0, 0)),
        pl.BlockSpec((None, Q_PER_KV, 1), lambda bi, g, pt, ix: (g, 0, 0))]
    gs = pltpu.PrefetchScalarGridSpec(
        num_scalar_prefetch=2, grid=(b, hk), in_specs=in_specs,
        out_specs=bgj(HEAD_DIM))
    return pl.pallas_call(
        _nsa_sample_attn_kernel, grid_spec=gs,
        out_shape=jax.ShapeDtypeStruct((b, hk, Q_PER_KV, HEAD_DIM), F32),
        compiler_params=_cp("parallel", "parallel"), name="nsa_sample_attn")(
            page_table.reshape(-1), idx.reshape(-1), *([csel] * N_OLD_SEL), cwin,
            kvs_new.reshape(b, hk, 1, 2 * HEAD_DIM), kvw_new.reshape(b, hk, 1, 2 * HEAD_DIM),
            q_s.reshape(b, hk, Q_PER_KV, HEAD_DIM), o_c.reshape(b, hk, Q_PER_KV, HEAD_DIM), gn_s, bs, bw, b0)


def _router_kernel(h_ref, w_ref, b_ref, o_ref):
    logits = _dot(h_ref[...], w_ref[...]) + b_ref[...]
    lane = lax.broadcasted_iota(jnp.int32, logits.shape, 1)
    lanef = lane.astype(F32)

    def softmax(mask):
        s = jnp.where(mask, logits, NEG)
        e = jnp.where(mask, jnp.exp(s - jnp.max(s, axis=-1, keepdims=True)), 0.0)
        return e / jnp.sum(e, axis=-1, keepdims=True)

    def top1(p, mask):
        mx = jnp.max(jnp.where(mask, p, -1.0), axis=-1, keepdims=True)
        idx = jnp.min(jnp.where(mask & (p == mx), lanef, 1e9), axis=-1, keepdims=True)
        return mx, idx

    gmask = (lane >= N_EXPERTS) & (lane < N_EXPERTS + MOE_GROUPS)
    pg = softmax(gmask)
    p_top, g_idx = top1(pg, gmask)
    e_lo = (g_idx - N_EXPERTS) * MOE_PER_GROUP
    emask = (lanef >= e_lo) & (lanef < e_lo + MOE_PER_GROUP)
    pe = softmax(emask)
    w_a, i_a = top1(pe, emask)
    mask_b = emask & (lanef != i_a)
    w_b, i_b = top1(pe, mask_b)
    tot = w_a + w_b
    gate = jnp.where(lanef == i_a, w_a / tot * p_top, jnp.where(lanef == i_b, w_b / tot * p_top, 0.0))
    o_ref[...] = gate


def _router(hn, w_group, b_group, w_exp, b_exp):
    w = jnp.concatenate([w_exp, w_group, jnp.zeros((D_MODEL, 128 - N_EXPERTS - MOE_GROUPS), F32)], axis=1)
    bias = jnp.concatenate([b_exp, b_group, jnp.zeros((128 - N_EXPERTS - MOE_GROUPS,), F32)]).reshape(1, 128)
    return pl.pallas_call(
        _router_kernel, grid=(M_ALL // TM,),
        in_specs=[pl.BlockSpec((TM, D_MODEL), lambda i: (i, 0)),
                  pl.BlockSpec((D_MODEL, 128), lambda i: (0, 0)),
                  pl.BlockSpec((1, 128), lambda i: (0, 0))],
        out_specs=pl.BlockSpec((TM, 128), lambda i: (i, 0)),
        out_shape=jax.ShapeDtypeStruct((M_ALL, 128), F32),
        compiler_params=_cp("parallel"), name="moe_router")(hn, w.astype(BF16), bias)


def _moe_kernel(h_ref, win_ref, wout_ref, gate_ref, x_ref, o_ref, acc_scr):
    e = pl.program_id(1)

    @pl.when(e == 0)
    def _():
        acc_scr[...] = jnp.zeros(acc_scr.shape, F32)

    u = _dot(h_ref[...], win_ref[...])
    gate = gate_ref[...]
    lane = lax.broadcasted_iota(jnp.int32, gate.shape, 1)
    ge = jnp.sum(jnp.where(lane == e, gate, 0.0), axis=-1, keepdims=True)
    act = _silu(u[:, :D_EXPERT]) * u[:, D_EXPERT:] * ge
    acc_scr[...] = acc_scr[...] + _dot(act.astype(BF16), wout_ref[...])

    @pl.when(e == N_EXPERTS - 1)
    def _():
        o_ref[...] = x_ref[...] + acc_scr[...]


def _moe(hn, gate, x, w_in_e, w_out_e):
    return pl.pallas_call(
        _moe_kernel, grid=(M_ALL // TM, N_EXPERTS),
        in_specs=[pl.BlockSpec((TM, D_MODEL), lambda i, e: (i, 0)),
                  pl.BlockSpec((None, D_MODEL, 2 * D_EXPERT), lambda i, e: (e, 0, 0)),
                  pl.BlockSpec((None, D_EXPERT, D_MODEL), lambda i, e: (e, 0, 0)),
                  pl.BlockSpec((TM, 128), lambda i, e: (i, 0)),
                  pl.BlockSpec((TM, D_MODEL), lambda i, e: (i, 0))],
        out_specs=pl.BlockSpec((TM, D_MODEL), lambda i, e: (i, 0)),
        out_shape=jax.ShapeDtypeStruct((M_ALL, D_MODEL), F32),
        scratch_shapes=[pltpu.VMEM((TM, D_MODEL), F32)],
        compiler_params=_cp("parallel", "arbitrary"), name="moe_experts")(hn, w_in_e, w_out_e, gate, x)


def _prep_w_in(w):
    offs = np.cumsum((0,) + IN_SIZES)
    z, xbc, dt, q, kvc, kvs, kvw, gn, gm = [w[:, offs[k]:offs[k + 1]] for k in range(len(IN_SIZES))]
    small = jnp.concatenate([dt, gn, jnp.zeros((D_MODEL, 256 - dt.shape[1] - gn.shape[1]), w.dtype)], axis=1)
    return jnp.concatenate([z, xbc, q, kvc, kvs, kvw, gm, small], axis=1).astype(BF16)


def _layer(l, x, p, cache_cmp_kv, cache_sel_kv, cache_win_kv, state_ssm_l, state_conv_l, page_table, rel_bias):
    s = SEQ
    b = DEC_BATCH
    h1 = _rmsnorm(x, p['norm1_w'], BF16)
    proj = _mm(h1, _prep_w_in(p['w_in']), tn=1792, out_dtype=F32, name="in_proj")

    kv_new = {name: proj[:, c:c + KV_COLS] for name, c in (('cmp', C_KVC), ('sel', C_KVS), ('win', C_KVW))}

    act_p = _conv_prompt(proj, p['conv_w'], p['conv_b'])
    bt = act_p[:, D_SSD:D_SSD + SSD_GROUPS * SSD_STATE].T.astype(BF16)
    dtr = proj[:, C_SMALL:C_SMALL + SSD_HEADS]
    y_p, ssm_p = _ssd_scan_prompt(act_p, bt, dtr[:s], p['dt_bias'], p['a_log'], p['d_skip'])
    conv_p = proj[s - (SSD_CONV - 1):s, C_XBC:C_XBC + CONV_CH]

    xbc_s = proj[s:, C_XBC:C_XBC + CONV_CH]
    act_s = _conv_sample(state_conv_l.transpose(1, 0, 2), xbc_s, p['conv_w'], p['conv_b'])
    state_t = state_ssm_l.reshape(b, D_SSD, SSD_STATE).transpose(0, 2, 1)
    y_s, h_s = _ssd_step_sample(state_t, act_s, dtr[s:], p['dt_bias'], p['a_log'], p['d_skip'])
    ssm_s = h_s.transpose(0, 2, 1).reshape(b, SSD_HEADS, SSD_HEADDIM, SSD_STATE)
    conv_s = jnp.concatenate([state_conv_l[:, 1:], xbc_s[:, None, :]], axis=1)

    yn = _gated_norm(y_p, y_s.reshape(b, D_SSD), proj, p['ssd_norm_w'])
    y_a = _mm(yn, p['w_ssd_out'].astype(BF16), tn=1024, out_dtype=F32, name="ssd_out")

    b1c = _cmp_bias(p['cmp_pe'], p['cmp_w1'], p['cmp_b1'])
    w1c = (p['cmp_w1'].reshape(2, 2, CMP_STRIDE * HEAD_DIM, HEAD_DIM).transpose(0, 2, 1, 3)
           .reshape(2, CMP_STRIDE * HEAD_DIM, 2 * HEAD_DIM).astype(BF16))
    w2 = p['cmp_w2'].astype(BF16)

    n_grp_p = s // (N_PAGES * PAGE_SIZE)
    proj_pages = proj.reshape(M_ALL // PAGE_SIZE, PAGE_SIZE, N_PROJ)
    pt_p = jnp.arange(n_grp_p * N_PAGES, dtype=jnp.int32)
    pp = _chunk_proj(proj_pages,
                     lambda pg: pl.BlockSpec((None, PAGE_SIZE, KV_COLS),
                                             lambda gi, pt: (pt[gi * N_PAGES + pg], 0, C_KVC // KV_COLS)),
                     pt_p, w1c, n_grp_p)
    kcv_p = _cmp_mlp(pp, b1c, w2, 1, F32)[0]
    kcv_p = jnp.pad(kcv_p, ((0, 0), (CMP_PAD, CMP_ROWS - CMP_PAD - kcv_p.shape[1]), (0, 0)))
    kvs_b = kv_new['sel'][:s].astype(BF16)
    kvw_b = kv_new['win'][:s].astype(BF16)
    attn_p = _nsa_prompt(proj, kcv_p, kvs_b, kvw_b, rel_bias, s)

    ccmp = cache_cmp_kv.reshape(DEPTH, -1, PAGE_SIZE, KV_COLS)
    ps = _chunk_proj(ccmp,
                     lambda pg: pl.BlockSpec((None, None, PAGE_SIZE, KV_COLS),
                                             lambda bi, pt: (l, pt[bi * N_PAGES + pg], 0, 0)),
                     page_table.reshape(-1), w1c, b)
    kcv_s = _cmp_mlp(ps, b1c, w2, b, BF16)
    q_s = proj[s:, C_Q:C_Q + NSA_HEADS * HEAD_DIM].reshape(b, NSA_HEADS, HEAD_DIM)
    o_c, idx = _nsa_sample_cmp(q_s, kcv_s, rel_bias)
    gn_s = (proj[s:, C_SMALL + GATE_LANE0:C_SMALL + GATE_LANE0 + 3 * NSA_HEADS]
            .reshape(b, 3, NSA_KV_HEADS, Q_PER_KV).transpose(0, 2, 3, 1))
    attn_s = _nsa_sample_attn(l, cache_sel_kv, cache_win_kv, page_table, idx[:, :NSA_KV_HEADS, :N_OLD_SEL],
                              kv_new['sel'][s:], kv_new['win'][s:], q_s, o_c, gn_s, rel_bias)
    attn = jnp.concatenate([attn_p, attn_s.reshape(b, NSA_HEADS * HEAD_DIM).astype(BF16)], axis=0)

    tn = 1024
    m = _mm(attn, p['w_nsa_out'].astype(BF16), tn=tn, out_dtype=BF16, name="nsa_out_merge",
            extras=((y_a, 0), (proj, C_GM // tn), (proj, (C_GM + D_MODEL) // tn)), epilogue=_merge_epilogue)
    x = _mm(m, p['w_out'].astype(BF16), tn=tn, out_dtype=F32, name="out_proj", extras=((x, 0),),
            epilogue=_resid_epilogue)
    h2 = _rmsnorm(x, p['norm2_w'], BF16)
    gate = _router(h2, p['w_router_group'], p['b_router_group'], p['w_router_exp'], p['b_router_exp'])
    x = _moe(h2, gate, x, p['w_exp_in'].astype(BF16), p['w_exp_out'].astype(BF16))

    kv5 = lambda a: a.reshape(a.shape[0], NSA_KV_HEADS, 2, HEAD_DIM)
    win_s = jnp.concatenate([cache_win_kv[l][:, 1:], kv5(kv_new['win'][s:])[:, None]], axis=1)
    outs = dict(
        cmp_p=kv5(kv_new['cmp'][:s])[None], cmp_s=kv5(kv_new['cmp'][s:])[:, None],
        sel_p=kv5(kv_new['sel'][:s])[None], sel_s=kv5(kv_new['sel'][s:])[:, None],
        win_p=kv5(kv_new['win'][s - WINDOW:s])[None], win_s=win_s,
        ssm_p=ssm_p[None], ssm_s=ssm_s, conv_p=conv_p[None], conv_s=conv_s)
    return x, outs


def kernel(x_prompt, x_sample, cache_cmp_kv, cache_sel_kv, cache_win_kv, state_ssm, state_conv, page_table,
           rel_bias, norm1_w, norm2_w, final_norm_w, w_in, conv_w, conv_b, dt_bias, a_log, d_skip, ssd_norm_w,
           w_ssd_out, cmp_pe, cmp_w1, cmp_b1, cmp_w2, w_nsa_out, w_out, w_router_group, b_router_group,
           w_router_exp, b_router_exp, w_exp_in, w_exp_out):
    x = jnp.concatenate([x_prompt[0], x_sample[:, 0]], axis=0)
    per_layer = []
    for l in range(DEPTH):
        p = {'norm1_w': norm1_w[l], 'norm2_w': norm2_w[l], 'w_in': w_in[l], 'conv_w': conv_w[l],
             'conv_b': conv_b[l], 'dt_bias': dt_bias[l], 'a_log': a_log[l], 'd_skip': d_skip[l],
             'ssd_norm_w': ssd_norm_w[l], 'w_ssd_out': w_ssd_out[l], 'cmp_pe': cmp_pe[l], 'cmp_w1': cmp_w1[l],
             'cmp_b1': cmp_b1[l], 'cmp_w2': cmp_w2[l], 'w_nsa_out': w_nsa_out[l], 'w_out': w_out[l],
             'w_router_group': w_router_group[l], 'b_router_group': b_router_group[l],
             'w_router_exp': w_router_exp[l], 'b_router_exp': b_router_exp[l],
             'w_exp_in': w_exp_in[l], 'w_exp_out': w_exp_out[l]}
        x, outs = _layer(l, x, p, cache_cmp_kv, cache_sel_kv, cache_win_kv, state_ssm[l], state_conv[l],
                         page_table, rel_bias)
        per_layer.append(outs)
    y = _rmsnorm(x, final_norm_w, F32)
    stack = lambda k: jnp.stack([o[k] for o in per_layer])
    return (y[:SEQ][None], y[SEQ:][:, None], stack('cmp_p'), stack('cmp_s'), stack('sel_p'), stack('sel_s'),
            stack('win_p'), stack('win_s'), stack('ssm_p'), stack('ssm_s'), stack('conv_p'), stack('conv_s'))
```

```python
import functools
import math

import numpy as np
import jax
import jax.numpy as jnp
from jax import lax
from jax.experimental import pallas as pl
from jax.experimental.pallas import tpu as pltpu

F32 = jnp.float32
BF16 = jnp.bfloat16
NEG = -1e30

D_MODEL = 2048
SEQ = 8192
DEPTH = 2
DEC_BATCH = 128
PAST_LEN = 2048
PAGE_SIZE = 128
N_PAGES = PAST_LEN // PAGE_SIZE
D_SSD = 4096
SSD_HEADDIM = 64
SSD_HEADS = 64
SSD_GROUPS = 8
SSD_STATE = 128
SSD_CONV = 4
SSD_CHUNK = 256
CONV_CH = 6144
NSA_HEADS = 16
NSA_KV_HEADS = 4
HEAD_DIM = 128
Q_PER_KV = 4
CMP_STRIDE = 16
SEL_BLOCK = 64
N_SEL = 16
WINDOW = 512
Q_BLOCK = 128
FORCED_SCORE = 1e4
REL_BUCKETS = 32
REL_MAX_DIST = 128
MOE_GROUPS = 4
MOE_PER_GROUP = 4
N_EXPERTS = 16
D_EXPERT = 512
RMS_EPS = 1e-6
KV_COLS = 1024
IN_SIZES = (D_SSD, CONV_CH, SSD_HEADS, NSA_HEADS * HEAD_DIM, KV_COLS, KV_COLS, KV_COLS, 3 * NSA_HEADS, 2 * D_MODEL)
SCALE = HEAD_DIM ** -0.5

C_Z = 0
C_XBC = 4096
C_Q = 10240
C_KVC = 12288
C_KVS = 13312
C_KVW = 14336
C_GM = 15360
C_SMALL = 19456
N_PROJ = 19712
GATE_LANE0 = 64

M_ALL = SEQ + DEC_BATCH
TM = 640
VMEM_LIMIT_MIB = 56


def _cp(*sem):
    return pltpu.CompilerParams(dimension_semantics=sem, vmem_limit_bytes=VMEM_LIMIT_MIB * 1024 * 1024)


def _dot(a, b):
    return jnp.dot(a, b, preferred_element_type=F32)


def _dot_nt(a, b):
    return lax.dot_general(a, b, (((1,), (1,)), ((), ())), preferred_element_type=F32)


def _dot_f32(a, b):
    return jnp.dot(a, b, preferred_element_type=F32, precision=lax.Precision.HIGHEST)


def _silu(x):
    return x * jax.nn.sigmoid(x)


def _softplus(x):
    return jnp.maximum(x, 0.0) + jnp.log1p(jnp.exp(-jnp.abs(x)))


def _split_hi_lo(v):
    hi = v.astype(BF16)
    lo = (v - hi.astype(F32)).astype(BF16)
    return hi, lo


def _expand(v, e):
    hi, lo = _split_hi_lo(v)
    return _dot(hi, e) + _dot(lo, e)


def _rmsnorm_kernel(x_ref, w_ref, o_ref):
    x = x_ref[...]
    ms = jnp.mean(x * x, axis=-1, keepdims=True)
    o_ref[...] = (x * lax.rsqrt(ms + RMS_EPS) * w_ref[...]).astype(o_ref.dtype)


def _rmsnorm(x, w, out_dtype, tm=TM):
    m, d = x.shape
    return pl.pallas_call(
        _rmsnorm_kernel, grid=(m // tm,),
        in_specs=[pl.BlockSpec((tm, d), lambda i: (i, 0)), pl.BlockSpec((1, d), lambda i: (0, 0))],
        out_specs=pl.BlockSpec((tm, d), lambda i: (i, 0)),
        out_shape=jax.ShapeDtypeStruct((m, d), out_dtype),
        compiler_params=_cp("parallel"), name="rmsnorm")(x, w.reshape(1, d))


def _gated_norm_kernel(yp_ref, ys_ref, z_ref, w_ref, o_ref, *, n_prompt_tiles):
    i = pl.program_id(0)
    y = jnp.where(i < n_prompt_tiles, yp_ref[...], ys_ref[...])
    gated = y * _silu(z_ref[...])
    ms = jnp.mean(gated * gated, axis=-1, keepdims=True)
    o_ref[...] = (gated * lax.rsqrt(ms + RMS_EPS) * w_ref[...]).astype(o_ref.dtype)


def _gated_norm(y_p, y_s, proj, w):
    tm = DEC_BATCH
    npt = SEQ // tm
    return pl.pallas_call(
        functools.partial(_gated_norm_kernel, n_prompt_tiles=npt), grid=(M_ALL // tm,),
        in_specs=[pl.BlockSpec((tm, D_SSD), lambda i: (jnp.minimum(i, npt - 1), 0)),
                  pl.BlockSpec((tm, D_SSD), lambda i: (0, 0)),
                  pl.BlockSpec((tm, D_SSD), lambda i: (i, C_Z // D_SSD)),
                  pl.BlockSpec((1, D_SSD), lambda i: (0, 0))],
        out_specs=pl.BlockSpec((tm, D_SSD), lambda i: (i, 0)),
        out_shape=jax.ShapeDtypeStruct((M_ALL, D_SSD), BF16),
        compiler_params=_cp("parallel"), name="gated_norm")(y_p, y_s, proj, w.reshape(1, D_SSD))


def _mm_kernel(a_ref, w_ref, *rest, epilogue):
    o_ref = rest[-1]
    acc = _dot(a_ref[...], w_ref[...])
    if epilogue is not None:
        acc = epilogue(acc, *[r[...] for r in rest[:-1]])
    o_ref[...] = acc.astype(o_ref.dtype)


def _mm(a, w, *, tn, out_dtype, name, extras=(), epilogue=None, tm=TM):
    m, k = a.shape
    n = w.shape[1]
    in_specs = [pl.BlockSpec((tm, k), lambda j, i: (i, 0)), pl.BlockSpec((k, tn), lambda j, i: (0, j))]
    args = [a, w]
    for arr, coff in extras:
        in_specs.append(pl.BlockSpec((tm, tn), lambda j, i, coff=coff: (i, coff + j)))
        args.append(arr)
    return pl.pallas_call(
        functools.partial(_mm_kernel, epilogue=epilogue), grid=(n // tn, m // tm),
        in_specs=in_specs, out_specs=pl.BlockSpec((tm, tn), lambda j, i: (i, j)),
        out_shape=jax.ShapeDtypeStruct((m, n), out_dtype),
        compiler_params=_cp("parallel", "parallel"), name=name)(*args)


def _merge_epilogue(y_b, y_a, g_a, g_b):
    return jax.nn.sigmoid(g_a) * y_a + jax.nn.sigmoid(g_b) * y_b


def _resid_epilogue(acc, x):
    return x + acc


CONV_TR = 512
CONV_TC = 512


def _conv_prompt_kernel(x_ref, halo_ref, w_ref, b_ref, o_ref):
    i = pl.program_id(1)
    halo = jnp.where(i > 0, halo_ref[...], 0.0)
    x = jnp.concatenate([halo, x_ref[...]], axis=0)
    n = x.shape[0]
    acc = b_ref[...] + x[8:] * w_ref[SSD_CONV - 1:SSD_CONV, :]
    for k in range(SSD_CONV - 1):
        shifted = pltpu.roll(x, SSD_CONV - 1 - k, axis=0)
        acc = acc + shifted[8:] * w_ref[k:k + 1, :]
    del n
    o_ref[...] = _silu(acc)


def _conv_prompt(proj, conv_w, conv_b):
    nct = CONV_CH // CONV_TC
    nrt = SEQ // CONV_TR
    c0 = C_XBC // CONV_TC
    return pl.pallas_call(
        _conv_prompt_kernel, grid=(nct, nrt),
        in_specs=[pl.BlockSpec((CONV_TR, CONV_TC), lambda c, i: (i, c0 + c)),
                  pl.BlockSpec((8, CONV_TC), lambda c, i: (jnp.maximum(i * (CONV_TR // 8) - 1, 0), c0 + c)),
                  pl.BlockSpec((SSD_CONV, CONV_TC), lambda c, i: (0, c)),
                  pl.BlockSpec((1, CONV_TC), lambda c, i: (0, c))],
        out_specs=pl.BlockSpec((CONV_TR, CONV_TC), lambda c, i: (i, c)),
        out_shape=jax.ShapeDtypeStruct((SEQ, CONV_CH), F32),
        compiler_params=_cp("parallel", "parallel"), name="conv_prompt")(
            proj, proj, conv_w, conv_b.reshape(1, CONV_CH))


GW = SSD_HEADS // SSD_GROUPS * SSD_HEADDIM
HPG = SSD_HEADS // SSD_GROUPS


def _scan_kernel(xs_ref, b_ref, c_ref, bt_ref, dtr_ref, dtrt_ref, dtb_r_ref, dtb_c_ref, al_r_ref, al_c_ref,
                 dskip_ref, e_ref, y_ref, hout_ref, h_scr):
    c = pl.program_id(1)
    q = SSD_CHUNK

    @pl.when(c == 0)
    def _():
        h_scr[...] = jnp.zeros((SSD_STATE, GW), F32)

    dt = _softplus(dtr_ref[...] + dtb_r_ref[...])
    d_a = dt * (-jnp.exp(al_r_ref[...]))
    dt_t = _softplus(dtrt_ref[...] + dtb_c_ref[...])
    d_a_t = dt_t * (-jnp.exp(al_c_ref[...]))
    row = lax.broadcasted_iota(jnp.int32, (q, q), 0)
    col = lax.broadcasted_iota(jnp.int32, (q, q), 1)
    tril = row >= col
    cum = _dot_f32(tril.astype(F32), d_a)
    cum_t = _dot_f32(d_a_t, (col >= row).astype(F32))
    cum_last = cum[q - 1:q, :]
    e = e_ref[...]
    dt_e = _expand(dt, e)
    ecum_e = _expand(jnp.exp(cum), e)
    tail_e = _expand(jnp.exp(cum_last - cum), e)
    elast_e = _expand(jnp.broadcast_to(jnp.exp(cum_last), (8, 128)), e)[0:1, :]

    xs = xs_ref[...]
    xdt = xs * dt_e
    xdt_b = xdt.astype(BF16)
    xw = (xdt * tail_e).astype(BF16)
    bg = b_ref[...].astype(BF16)
    cg = c_ref[...].astype(BF16)
    cb = _dot_nt(cg, bg)
    h_prev = h_scr[...]
    y = _dot(cg, h_prev.astype(BF16)) * ecum_e
    lane = lax.broadcasted_iota(jnp.int32, (q, 128), 1)
    ys = []
    for pair in range(HPG // 2):
        xpair = xdt_b[:, pair * 128:(pair + 1) * 128]
        top = jnp.where(lane < SSD_HEADDIM, xpair, jnp.zeros_like(xpair))
        bot = jnp.where(lane >= SSD_HEADDIM, xpair, jnp.zeros_like(xpair))
        atts = []
        for hh in range(2):
            h8 = pair * 2 + hh
            seg = jnp.exp(jnp.where(tril, cum[:, h8:h8 + 1] - cum_t[h8:h8 + 1, :], NEG))
            atts.append((cb * seg).astype(BF16))
        ys.append(_dot(jnp.concatenate(atts, axis=1), jnp.concatenate([top, bot], axis=0)))
    y = y + jnp.concatenate(ys, axis=1) + dskip_ref[...] * xs
    y_ref[...] = y
    h_new = h_prev * elast_e + _dot(bt_ref[...], xw)
    h_scr[...] = h_new

    @pl.when(c == pl.num_programs(1) - 1)
    def _():
        hout_ref[...] = h_new


def _head_expand_matrix(n_heads, rows):
    e = np.zeros((rows, n_heads * SSD_HEADDIM), np.float32)
    for h in range(n_heads):
        e[h, h * SSD_HEADDIM:(h + 1) * SSD_HEADDIM] = 1.0
    return jnp.asarray(e, BF16)


def _ssd_scan_prompt(act, bt, dtr, dt_bias, a_log, d_skip):
    q = SSD_CHUNK
    nc = SEQ // q
    dtr_g = dtr.reshape(SEQ, SSD_GROUPS, HPG).transpose(1, 0, 2)
    dtr_g128 = jnp.pad(dtr_g, ((0, 0), (0, 0), (0, 128 - HPG)))
    dtrt_g = dtr_g.transpose(0, 2, 1)
    pad_r = lambda v: jnp.pad(v.reshape(SSD_GROUPS, 1, HPG), ((0, 0), (0, 0), (0, 128 - HPG)))
    col_c = lambda v: v.reshape(SSD_GROUPS, HPG, 1)
    dskip_e = jnp.repeat(d_skip, SSD_HEADDIM).reshape(1, D_SSD)
    e = _head_expand_matrix(HPG, 128)
    small = lambda shape: pl.BlockSpec((None,) + shape, lambda g, c: (g, 0, 0))
    y, h_out = pl.pallas_call(
        _scan_kernel, grid=(SSD_GROUPS, nc),
        in_specs=[pl.BlockSpec((q, GW), lambda g, c: (c, g)),
                  pl.BlockSpec((q, SSD_STATE), lambda g, c: (c, D_SSD // SSD_STATE + g)),
                  pl.BlockSpec((q, SSD_STATE), lambda g, c: (c, (D_SSD + SSD_GROUPS * SSD_STATE) // SSD_STATE + g)),
                  pl.BlockSpec((SSD_STATE, q), lambda g, c: (g, c)),
                  pl.BlockSpec((None, q, 128), lambda g, c: (g, c, 0)),
                  pl.BlockSpec((None, HPG, q), lambda g, c: (g, 0, c)),
                  small((1, 128)), small((HPG, 1)), small((1, 128)), small((HPG, 1)),
                  pl.BlockSpec((1, GW), lambda g, c: (0, g)),
                  pl.BlockSpec((128, GW), lambda g, c: (0, 0))],
        out_specs=[pl.BlockSpec((q, GW), lambda g, c: (c, g)),
                   pl.BlockSpec((None, SSD_STATE, GW), lambda g, c: (g, 0, 0))],
        out_shape=[jax.ShapeDtypeStruct((SEQ, D_SSD), F32),
                   jax.ShapeDtypeStruct((SSD_GROUPS, SSD_STATE, GW), F32)],
        scratch_shapes=[pltpu.VMEM((SSD_STATE, GW), F32)],
        compiler_params=_cp("parallel", "arbitrary"), name="ssd_scan")(
            act, act, act, bt, dtr_g128, dtrt_g, pad_r(dt_bias), col_c(dt_bias), pad_r(a_log), col_c(a_log),
            dskip_e, e)
    h_last = h_out.reshape(SSD_GROUPS, SSD_STATE, HPG, SSD_HEADDIM).transpose(0, 2, 3, 1)
    return y, h_last.reshape(SSD_HEADS, SSD_HEADDIM, SSD_STATE)


def _conv_sample_kernel(s_ref, x_ref, w_ref, b_ref, o_ref):
    acc = b_ref[...] + x_ref[...] * w_ref[SSD_CONV - 1:SSD_CONV, :]
    for k in range(SSD_CONV - 1):
        acc = acc + s_ref[k] * w_ref[k:k + 1, :]
    o_ref[...] = _silu(acc)


def _conv_sample(state_t, xbc, conv_w, conv_b):
    tc = 1024
    return pl.pallas_call(
        _conv_sample_kernel, grid=(CONV_CH // tc,),
        in_specs=[pl.BlockSpec((SSD_CONV - 1, DEC_BATCH, tc), lambda c: (0, 0, c)),
                  pl.BlockSpec((DEC_BATCH, tc), lambda c: (0, c)),
                  pl.BlockSpec((SSD_CONV, tc), lambda c: (0, c)),
                  pl.BlockSpec((1, tc), lambda c: (0, c))],
        out_specs=pl.BlockSpec((DEC_BATCH, tc), lambda c: (0, c)),
        out_shape=jax.ShapeDtypeStruct((DEC_BATCH, CONV_CH), F32),
        compiler_params=_cp("parallel"), name="conv_sample")(state_t, xbc, conv_w, conv_b.reshape(1, CONV_CH))


def _ssd_step_kernel(h_ref, xs_ref, dtr_ref, bn_ref, cn_ref, dtb_ref, al_ref, dskip_ref, e64_ref, e8_ref,
                     y_ref, hout_ref):
    dt = _softplus(dtr_ref[...] + dtb_ref[...])
    dec = jnp.exp(dt * (-jnp.exp(al_ref[...])))
    e64 = e64_ref[...]
    dt_e = _expand(dt, e64)[0:1, :]
    dec_e = _expand(dec, e64)[0:1, :]
    xs = xs_ref[...]
    xdt = xs * dt_e
    e8 = e8_ref[...]
    b_exp = _dot(bn_ref[...].astype(BF16), e8)
    c_exp = _dot(cn_ref[...].astype(BF16), e8)
    h_new = h_ref[...] * dec_e + b_exp * xdt
    hout_ref[...] = h_new
    y_ref[...] = jnp.sum(h_new * c_exp, axis=0, keepdims=True) + dskip_ref[...] * xs


def _ssd_step_sample(state_t, act_s, dtr_s, dt_bias, a_log, d_skip):
    b = DEC_BATCH
    xs3 = act_s[:, :D_SSD].reshape(b, 1, D_SSD)
    bn = act_s[:, D_SSD:D_SSD + 1024].reshape(b, SSD_GROUPS, SSD_STATE).transpose(0, 2, 1)
    cn = act_s[:, D_SSD + 1024:].reshape(b, SSD_GROUPS, SSD_STATE).transpose(0, 2, 1)
    bn = jnp.pad(bn, ((0, 0), (0, 0), (0, 128 - SSD_GROUPS)))
    cn = jnp.pad(cn, ((0, 0), (0, 0), (0, 128 - SSD_GROUPS)))
    dtr8 = jnp.pad(dtr_s.reshape(b, 1, SSD_HEADS), ((0, 0), (0, 7), (0, 128 - SSD_HEADS)))
    row128 = lambda v: jnp.pad(v.reshape(1, SSD_HEADS), ((0, 0), (0, 128 - SSD_HEADS)))
    dskip_e = jnp.repeat(d_skip, SSD_HEADDIM).reshape(1, D_SSD)
    e64 = _head_expand_matrix(SSD_HEADS, 128)
    e8np = np.zeros((128, D_SSD), np.float32)
    for g in range(SSD_GROUPS):
        e8np[g, g * GW:(g + 1) * GW] = 1.0
    e8 = jnp.asarray(e8np, BF16)
    full = lambda shape: pl.BlockSpec(shape, lambda i: (0,) * len(shape))
    per_b = lambda shape: pl.BlockSpec((None,) + shape, lambda i: (i, 0, 0))
    return pl.pallas_call(
        _ssd_step_kernel, grid=(b,),
        in_specs=[per_b((SSD_STATE, D_SSD)), per_b((1, D_SSD)), per_b((8, 128)),
                  per_b((SSD_STATE, 128)), per_b((SSD_STATE, 128)),
                  full((1, 128)), full((1, 128)), full((1, D_SSD)), full((128, D_SSD)), full((128, D_SSD))],
        out_specs=[per_b((1, D_SSD)), per_b((SSD_STATE, D_SSD))],
        out_shape=[jax.ShapeDtypeStruct((b, 1, D_SSD), F32), jax.ShapeDtypeStruct((b, SSD_STATE, D_SSD), F32)],
        compiler_params=_cp("parallel"), name="ssd_step")(
            state_t, xs3, dtr8, bn, cn, row128(dt_bias), row128(a_log), dskip_e, e64, e8)


def _chunk_proj_kernel(pt_ref, *refs):
    del pt_ref
    pages = refs[:N_PAGES]
    w_ref, o_ref = refs[N_PAGES:]
    nch = N_PAGES * PAGE_SIZE // CMP_STRIDE
    cpp = PAGE_SIZE // CMP_STRIDE
    for s in range(2):
        lhs_h = []
        for h in range(NSA_KV_HEADS):
            per_r = []
            for r in range(CMP_STRIDE):
                rows = [pg[pl.ds(r, cpp, stride=CMP_STRIDE), h, s, :] for pg in pages]
                per_r.append(jnp.concatenate(rows, axis=0).astype(BF16))
            lhs_h.append(jnp.concatenate(per_r, axis=1))
        res = _dot(jnp.concatenate(lhs_h, axis=0), w_ref[s])
        for h in range(NSA_KV_HEADS):
            o_ref[s, h] = res[h * nch:(h + 1) * nch]


def _chunk_proj(pages_arr, layer, page_table, w1c, n_groups):
    nch = N_PAGES * PAGE_SIZE // CMP_STRIDE

    def page_spec(pg):
        return pl.BlockSpec((None, None, PAGE_SIZE, NSA_KV_HEADS, 2, HEAD_DIM),
                            lambda b, pt: (layer, pt[b * N_PAGES + pg], 0, 0, 0, 0))

    in_specs = [page_spec(p) for p in range(N_PAGES)]
    in_specs.append(pl.BlockSpec((2, CMP_STRIDE * HEAD_DIM, 2 * HEAD_DIM), lambda b, pt: (0, 0, 0)))
    gs = pltpu.PrefetchScalarGridSpec(
        num_scalar_prefetch=1, grid=(n_groups,), in_specs=in_specs,
        out_specs=pl.BlockSpec((2, NSA_KV_HEADS, nch, 2 * HEAD_DIM), lambda b, pt: (0, 0, b, 0)))
    return pl.pallas_call(
        _chunk_proj_kernel, grid_spec=gs,
        out_shape=jax.ShapeDtypeStruct((2, NSA_KV_HEADS, n_groups * nch, 2 * HEAD_DIM), F32),
        compiler_params=_cp("parallel"), name="cmp_chunk_proj")(page_table, *([pages_arr] * N_PAGES), w1c)


def _cmp_bias_kernel(pe_ref, w_ref, b_ref, o_ref):
    o_ref[...] = b_ref[...] + _dot(pe_ref[...], w_ref[...])


def _cmp_bias(cmp_pe, cmp_w1, cmp_b1):
    k = 2 * CMP_STRIDE * HEAD_DIM
    pe = jnp.broadcast_to(cmp_pe.reshape(2, 1, k), (2, 8, k)).astype(BF16)
    w = cmp_w1.reshape(2, k, HEAD_DIM).astype(BF16)
    b = jnp.broadcast_to(cmp_b1.reshape(2, 1, HEAD_DIM), (2, 8, HEAD_DIM))
    return pl.pallas_call(
        _cmp_bias_kernel, grid=(2,),
        in_specs=[pl.BlockSpec((None, 8, k), lambda s: (s, 0, 0)),
                  pl.BlockSpec((None, k, HEAD_DIM), lambda s: (s, 0, 0)),
                  pl.BlockSpec((None, 8, HEAD_DIM), lambda s: (s, 0, 0))],
        out_specs=pl.BlockSpec((None, 8, HEAD_DIM), lambda s: (s, 0, 0)),
        out_shape=jax.ShapeDtypeStruct((2, 8, HEAD_DIM), F32),
        compiler_params=_cp("parallel"), name="cmp_bias")(pe, w, b)


def _cmp_mlp_kernel(p_ref, b_ref, w2_ref, o_ref):
    n = p_ref.shape[2]
    for s in range(2):
        for h in range(NSA_KV_HEADS):
            p = p_ref[s, h]
            nxt = pltpu.roll(p[:, HEAD_DIM:], n - 1, axis=0)
            hid = _silu(p[:, :HEAD_DIM] + nxt + b_ref[s, 0:1, :])
            o_ref[h * 2 + s] = _dot(hid.astype(BF16), w2_ref[s]).astype(o_ref.dtype)


def _cmp_mlp(p, b1c, w2, n_groups, out_dtype):
    nch = p.shape[2] // n_groups
    return pl.pallas_call(
        _cmp_mlp_kernel, grid=(n_groups,),
        in_specs=[pl.BlockSpec((2, NSA_KV_HEADS, nch, 2 * HEAD_DIM), lambda b: (0, 0, b, 0)),
                  pl.BlockSpec((2, 8, HEAD_DIM), lambda b: (0, 0, 0)),
                  pl.BlockSpec((2, HEAD_DIM, HEAD_DIM), lambda b: (0, 0, 0))],
        out_specs=pl.BlockSpec((None, 2 * NSA_KV_HEADS, nch, HEAD_DIM), lambda b: (b, 0, 0, 0)),
        out_shape=jax.ShapeDtypeStruct((n_groups, 2 * NSA_KV_HEADS, nch, HEAD_DIM), out_dtype),
        compiler_params=_cp("parallel"), name="cmp_mlp")(p, b1c, w2)


def _bucket_np(dist):
    n = np.maximum(dist, 0)
    exact = REL_BUCKETS // 2
    nf = np.maximum(n, 1).astype(np.float32)
    big = exact + (np.log(nf / np.float32(exact)) / np.float32(math.log(REL_MAX_DIST / exact))
                   * np.float32(REL_BUCKETS - exact)).astype(np.int32)
    return np.where(n < exact, n, np.minimum(big, REL_BUCKETS - 1))


def _bias_table(rel_bias_t, dist, valid):
    tab = rel_bias_t[:, _bucket_np(dist)]
    return jnp.where(jnp.asarray(valid), tab, NEG)


CMP_PAD = 120
CMP_ROWS = 768


def _wsel_np(n_cmp, n_blocks, rows, pad):
    w = np.zeros((rows, 128), np.float32)
    c0 = np.arange(n_cmp)[:, None] * CMP_STRIDE
    s0 = np.arange(n_blocks)[None, :] * SEL_BLOCK
    inter = np.minimum(c0 + 2 * CMP_STRIDE, s0 + SEL_BLOCK) - np.maximum(c0, s0)
    w[pad:pad + n_cmp, :n_blocks] = np.maximum(inter, 0).astype(np.float32) / (2 * CMP_STRIDE)
    return w


M0 = -1e20


def _nsa_prompt_kernel(q_ref, kcw_ref, vcw_ref, wselw_ref, kco_ref, vco_ref, wselo_ref,
                       ks_ref, vs_ref, kw_ref, vw_ref, small_ref, tc_ref, td_ref, tp_ref, cb_ref,
                       o_ref, m_scr, l_scr, acc_scr, imp_scr, out_scr, sel_scr, gate_scr):
    g = pl.program_id(0)
    i = pl.program_id(1)
    qb = Q_BLOCK
    w = Q_PER_KV * qb
    qf = q_ref[...]
    qt = jnp.concatenate([qf[:, j * HEAD_DIM:(j + 1) * HEAD_DIM].T.astype(BF16) for j in range(Q_PER_KV)],
                         axis=1)
    row = lax.broadcasted_iota(jnp.int32, (qb, qb), 0)
    col = lax.broadcasted_iota(jnp.int32, (qb, qb), 1)
    row4 = lax.broadcasted_iota(jnp.int32, (qb, w), 0)
    cb = cb_ref[...]
    tile4 = lambda x: jnp.concatenate([x] * Q_PER_KV, axis=1)

    gate_scr[...] = jax.nn.sigmoid(small_ref[...]).T

    def gate_row(branch):
        start = GATE_LANE0 + branch * NSA_HEADS + g * Q_PER_KV
        r8 = gate_scr[pl.ds(pl.multiple_of((start // 8) * 8, 8), 8), :]
        r4 = jnp.where(start % 8 == 0, r8[0:Q_PER_KV], r8[Q_PER_KV:2 * Q_PER_KV])
        return jnp.concatenate([r4[j:j + 1, :] for j in range(Q_PER_KV)], axis=1)

    def reset():
        m_scr[...] = jnp.full(m_scr.shape, M0, F32)
        l_scr[...] = jnp.zeros(l_scr.shape, F32)
        acc_scr[...] = jnp.zeros(acc_scr.shape, F32)

    def flash_tile(k, vt, bias, mask, wsel_t=None):
        s = _dot(k, qt) * SCALE + bias
        if mask is not None:
            s = jnp.where(mask, s, NEG)
        m_prev = m_scr[...]
        m_new = jnp.maximum(m_prev, jnp.max(s, axis=0, keepdims=True))
        p = jnp.exp(s - m_new)
        alpha = jnp.exp(m_prev - m_new)
        l_scr[...] = alpha * l_scr[...] + jnp.sum(p, axis=0, keepdims=True)
        m_scr[...] = m_new
        pb = p.astype(BF16)
        acc_scr[...] = acc_scr[...] * alpha + _dot(vt, pb)
        if wsel_t is not None:
            imp_scr[...] = imp_scr[...] * alpha + _dot(wsel_t, pb)

    def normalized(ref):
        return ref[...] / jnp.maximum(l_scr[...], 1e-30)

    reset()
    imp_scr[...] = jnp.zeros(imp_scr.shape, F32)
    per_q = qb // CMP_STRIDE

    def cmp_body(kt, carry):
        npos = kt * qb + row4
        flash_tile(kco_ref[kt], vco_ref[kt], cb, (npos >= CMP_PAD) & (npos < i * per_q), wselo_ref[kt])
        return carry

    lax.fori_loop(0, (i * per_q + qb - 1) // qb, cmp_body, 0)
    flash_tile(kcw_ref[...], vcw_ref[...], tc_ref[...], (row4 + i * per_q) >= CMP_PAD, wselw_ref[...])
    out_scr[...] = gate_row(0) * normalized(acc_scr)
    impn = normalized(imp_scr)
    imp = impn[:, 0:qb]
    for j in range(1, Q_PER_KV):
        imp = imp + impn[:, j * qb:(j + 1) * qb]

    rowf = row.astype(F32)
    cur = 2 * i + (col >= SEL_BLOCK).astype(jnp.int32)
    forced = (row == 0) | (row == cur) | (row == cur - 1)
    score = jnp.where(forced, FORCED_SCORE, jnp.where(row <= cur, imp, -1.0))
    sel = jnp.zeros((qb, qb), F32)
    for _ in range(N_SEL):
        mx = jnp.max(score, axis=0, keepdims=True)
        idx = jnp.min(jnp.where(score == mx, rowf, 1e9), axis=0, keepdims=True)
        hit = rowf == idx
        sel = jnp.where(hit, 1.0, sel)
        score = jnp.where(hit, -3e38, score)
    sel_scr[...] = sel

    def sel_mask(kt):
        r8 = sel_scr[pl.ds(pl.multiple_of((kt // 4) * 8, 8), 8), :]
        o = kt % 4
        r2 = jnp.where(o == 0, r8[0:2], jnp.where(o == 1, r8[2:4], jnp.where(o == 2, r8[4:6], r8[6:8])))
        mval = jnp.where(row < SEL_BLOCK, r2[0:1, :], r2[1:2, :])
        return tile4(mval) > 0.5

    reset()

    def sel_body(kt, carry):
        flash_tile(ks_ref[kt], vs_ref[kt], cb, sel_mask(kt))
        return carry

    lax.fori_loop(0, jnp.maximum(i - 1, 0), sel_body, 0)

    @pl.when(i >= 1)
    def _():
        flash_tile(ks_ref[i - 1], vs_ref[i - 1], tp_ref[...], sel_mask(i - 1))

    flash_tile(ks_ref[i], vs_ref[i], td_ref[...], sel_mask(i))
    out_scr[...] = out_scr[...] + gate_row(1) * normalized(acc_scr)

    reset()
    upper = tile4((row > col).astype(F32)) > 0.5
    for back in range(WINDOW // qb, -1, -1):
        @pl.when(i >= back)
        def _(back=back):
            kt = i - back
            if back == WINDOW // qb:
                flash_tile(kw_ref[kt], vw_ref[kt], cb, upper)
            elif back >= 2:
                flash_tile(kw_ref[kt], vw_ref[kt], cb, None)
            elif back == 1:
                flash_tile(kw_ref[kt], vw_ref[kt], tp_ref[...], None)
            else:
                flash_tile(kw_ref[kt], vw_ref[kt], td_ref[...], None)
    out = out_scr[...] + gate_row(2) * normalized(acc_scr)
    for j in range(Q_PER_KV):
        o_ref[:, j * HEAD_DIM:(j + 1) * HEAD_DIM] = out[:, j * qb:(j + 1) * qb].T.astype(o_ref.dtype)


def _nsa_prompt(proj, kcv, kv_sel, kv_win, rel_bias, seq):
    qb = Q_BLOCK
    nb = seq // qb
    hk = NSA_KV_HEADS
    w = Q_PER_KV * qb

    def key_tiles(x):
        x5 = x.astype(BF16).reshape(nb, qb, hk, 2, HEAD_DIM)
        return x5[:, :, :, 0, :].transpose(2, 0, 1, 3), x5[:, :, :, 1, :].transpose(2, 0, 3, 1)

    ks, vs = key_tiles(kv_sel)
    kw, vw = key_tiles(kv_win)

    nck = kcv.shape[1]
    padded = jnp.pad(kcv.astype(BF16), ((0, 0), (CMP_PAD, CMP_ROWS - CMP_PAD - nck), (0, 0)))
    kc_pad, vc_pad = padded[0::2], padded[1::2]
    n_old = 4 * qb
    kco = kc_pad[:, :n_old].reshape(hk, 4, qb, HEAD_DIM)
    vco = vc_pad[:, :n_old].reshape(hk, 4, qb, HEAD_DIM).transpose(0, 1, 3, 2)
    widx = (qb // CMP_STRIDE) * np.arange(nb)[:, None] + np.arange(qb)[None, :]
    kcw = kc_pad[:, widx]
    vcw = vc_pad[:, widx].transpose(0, 1, 3, 2)
    wsel = _wsel_np(seq // CMP_STRIDE - 1, seq // SEL_BLOCK, CMP_ROWS, CMP_PAD)
    wselo = jnp.asarray(wsel[:n_old].reshape(4, qb, 128).transpose(0, 2, 1), BF16)
    wselw = jnp.asarray(wsel[widx].transpose(0, 2, 1), BF16)

    rbt = rel_bias.T
    a = np.arange(qb)[:, None]
    b = np.arange(qb)[None, :]
    lanes = lambda t16: t16.reshape(hk, Q_PER_KV, qb, qb).transpose(0, 3, 1, 2).reshape(hk, qb, w)
    td = lanes(_bias_table(rbt, a - b, a >= b))
    tp = lanes(_bias_table(rbt, a - b + qb, np.ones((qb, qb), bool)))
    dist_c = a - CMP_STRIDE * b + (CMP_STRIDE * CMP_PAD - 2 * CMP_STRIDE + 1)
    tc = lanes(_bias_table(rbt, dist_c, dist_c >= 0))
    cb = jnp.repeat(rel_bias[REL_BUCKETS - 1].reshape(hk, Q_PER_KV), qb, axis=1).reshape(hk, 1, w)

    per_g = lambda shape: pl.BlockSpec((None,) + shape, lambda g, i: (g,) + (0,) * len(shape))
    per_gi = pl.BlockSpec((None, None, qb, qb), lambda g, i: (g, i, 0, 0))
    return pl.pallas_call(
        _nsa_prompt_kernel, grid=(hk, nb),
        in_specs=[pl.BlockSpec((qb, w), lambda g, i: (i, C_Q // w + g)),
                  per_gi, per_gi, pl.BlockSpec((None, qb, qb), lambda g, i: (i, 0, 0)),
                  per_g((4, qb, qb)), per_g((4, qb, qb)), pl.BlockSpec((4, qb, qb), lambda g, i: (0, 0, 0)),
                  per_g((nb, qb, qb)), per_g((nb, qb, qb)), per_g((nb, qb, qb)), per_g((nb, qb, qb)),
                  pl.BlockSpec((qb, 256), lambda g, i: (i, C_SMALL // 256)),
                  per_g((qb, w)), per_g((qb, w)), per_g((qb, w)), per_g((1, w))],
        out_specs=pl.BlockSpec((qb, w), lambda g, i: (i, g)),
        out_shape=jax.ShapeDtypeStruct((seq, NSA_HEADS * HEAD_DIM), BF16),
        scratch_shapes=[pltpu.VMEM((1, w), F32), pltpu.VMEM((1, w), F32), pltpu.VMEM((HEAD_DIM, w), F32),
                        pltpu.VMEM((qb, w), F32), pltpu.VMEM((HEAD_DIM, w), F32), pltpu.VMEM((qb, qb), F32),
                        pltpu.VMEM((256, qb), F32)],
        compiler_params=_cp("parallel", "arbitrary"), name="nsa_prompt")(
            proj, kcw, vcw, wselw, kco, vco, wselo, ks, vs, kw, vw, proj, tc, td, tp, cb)


N_BLK_S = -(-(PAST_LEN + 1) // SEL_BLOCK)
N_OLD_SEL = N_SEL - 1


def _nsa_sample_cmp_kernel(q_ref, kcv_ref, bias_ref, wsel_ref, oc_ref, idx_ref):
    hrow = lax.broadcasted_iota(jnp.int32, (NSA_HEADS, 128), 0) // Q_PER_KV
    psums = []
    bias = bias_ref[...]
    valid = bias > 0.5 * NEG
    for bb in range(q_ref.shape[0]):
        q = q_ref[bb].astype(BF16)
        o_c = jnp.zeros((NSA_HEADS, HEAD_DIM), F32)
        for g in range(NSA_KV_HEADS):
            s = jnp.where(valid, _dot_nt(q, kcv_ref[bb, 2 * g]) * SCALE + bias, NEG)
            mx = jnp.max(s, axis=-1, keepdims=True)
            e = jnp.where(valid, jnp.exp(s - mx), 0.0)
            p = e / jnp.maximum(jnp.sum(e, axis=-1, keepdims=True), 1e-30)
            mine = hrow == g
            o_c = o_c + jnp.where(mine, _dot(p.astype(BF16), kcv_ref[bb, 2 * g + 1]), 0.0)
            psums.append(jnp.sum(jnp.where(mine, p, 0.0), axis=0, keepdims=True))
        oc_ref[bb] = o_c
    psum = jnp.concatenate(psums, axis=0)
    imp = _dot(psum.astype(BF16), wsel_ref[...])
    lane = lax.broadcasted_iota(jnp.int32, imp.shape, 1)
    cur = N_BLK_S - 1
    forced = (lane == 0) | (lane == cur - 1)
    score = jnp.where(lane >= cur, -3e38, jnp.where(forced, FORCED_SCORE, imp))
    lanef = lane.astype(F32)
    out = jnp.zeros(imp.shape, F32)
    for r in range(N_OLD_SEL):
        mx = jnp.max(score, axis=-1, keepdims=True)
        idx = jnp.min(jnp.where(score == mx, lanef, 1e9), axis=-1, keepdims=True)
        out = jnp.where(lane == r, idx, out)
        score = jnp.where(lanef == idx, -3e38, score)
    idx_ref[...] = out.astype(jnp.int32)


def _nsa_sample_cmp(q_s, kcv_s, rel_bias):
    b = DEC_BATCH
    rbt = rel_bias.T
    n = np.arange(128)[None, :]
    dist = PAST_LEN - (CMP_STRIDE * n + 2 * CMP_STRIDE - 1)
    bias = _bias_table(rbt, dist, n < (PAST_LEN // CMP_STRIDE - 1))[:, 0, :]
    wsel = jnp.asarray(_wsel_np(PAST_LEN // CMP_STRIDE - 1, N_BLK_S, 128, 0), BF16)
    bb = 8
    return pl.pallas_call(
        _nsa_sample_cmp_kernel, grid=(b // bb,),
        in_specs=[pl.BlockSpec((bb, NSA_HEADS, HEAD_DIM), lambda i: (i, 0, 0)),
                  pl.BlockSpec((bb, 2 * NSA_KV_HEADS, 128, HEAD_DIM), lambda i: (i, 0, 0, 0)),
                  pl.BlockSpec((NSA_HEADS, 128), lambda i: (0, 0)),
                  pl.BlockSpec((128, 128), lambda i: (0, 0))],
        out_specs=[pl.BlockSpec((bb, NSA_HEADS, HEAD_DIM), lambda i: (i, 0, 0)),
                   pl.BlockSpec((bb * NSA_KV_HEADS, 128), lambda i: (i, 0))],
        out_shape=[jax.ShapeDtypeStruct((b, NSA_HEADS, HEAD_DIM), F32),
                   jax.ShapeDtypeStruct((b * NSA_KV_HEADS, 128), jnp.int32)],
        compiler_params=_cp("parallel"), name="nsa_sample_cmp")(q_s, kcv_s, bias, wsel)


def _nsa_sample_attn_kernel(pt_ref, idx_ref, *refs):
    del pt_ref
    blocks = refs[:N_OLD_SEL]
    (win_ref, news_ref, neww_ref, q_ref, oc_ref, gn_ref, bs_ref, bw_ref, b0_ref, o_ref) = refs[N_OLD_SEL:]
    b = pl.program_id(0)
    g = pl.program_id(1)
    q = q_ref[...].astype(BF16)
    gates = jax.nn.sigmoid(gn_ref[...])
    b0 = b0_ref[...]

    def attend(pieces, new_ref):
        new = new_ref[...]
        s_new = jnp.sum(q.astype(F32) * new[:, :HEAD_DIM].astype(BF16).astype(F32), axis=-1, keepdims=True)
        s_new = s_new * SCALE + b0
        mx = s_new
        for s, valid, _ in pieces:
            mx = jnp.maximum(mx, jnp.max(s, axis=-1, keepdims=True))
        e_new = jnp.exp(s_new - mx)
        den = e_new
        acc = e_new.astype(BF16).astype(F32) * new[:, HEAD_DIM:].astype(BF16).astype(F32)
        for s, valid, v in pieces:
            e = jnp.exp(s - mx)
            if valid is not None:
                e = jnp.where(valid, e, 0.0)
            den = den + jnp.sum(e, axis=-1, keepdims=True)
            acc = acc + _dot(e.astype(BF16), v)
        return acc / jnp.maximum(den, 1e-30)

    pieces = []
    for kk in range(N_OLD_SEL):
        blk = blocks[kk]
        m = idx_ref[(b * NSA_KV_HEADS + g) * N_OLD_SEL + kk]
        bias = bs_ref[m]
        s = _dot_nt(q, blk[:, 0, :].astype(BF16)) * SCALE + bias
        pieces.append((s, None, blk[:, 1, :].astype(BF16)))
    o_s = attend(pieces, news_ref)

    bw = bw_ref[...]
    valid_w = bw > 0.5 * NEG
    s_w = jnp.where(valid_w, _dot_nt(q, win_ref[:, 0, :].astype(BF16)) * SCALE + bw, NEG)
    o_w = attend([(s_w, valid_w, win_ref[:, 1, :].astype(BF16))], neww_ref)

    out = gates[:, 0:1] * oc_ref[...] + gates[:, 1:2] * o_s + gates[:, 2:3] * o_w
    o_ref[...] = out.astype(o_ref.dtype)


def _nsa_sample_attn(layer, cache_sel, cache_win, page_table, idx, kvs_new, kvw_new, q_s, o_c, gn_s, rel_bias):
    b = DEC_BATCH
    hk = NSA_KV_HEADS
    rbt = rel_bias.T
    m = np.arange(N_BLK_S - 1)[:, None]
    r = np.arange(SEL_BLOCK)[None, :]
    bs = _bias_table(rbt, PAST_LEN - (SEL_BLOCK * m + r), np.ones((N_BLK_S - 1, SEL_BLOCK), bool))
    bs = bs.reshape(hk, Q_PER_KV, N_BLK_S - 1, SEL_BLOCK).transpose(0, 2, 1, 3)
    rw = np.arange(WINDOW)[None, :]
    bw = _bias_table(rbt, WINDOW - rw, rw >= 1)[:, 0, :].reshape(hk, Q_PER_KV, WINDOW)
    b0 = rel_bias[0].reshape(hk, Q_PER_KV, 1)
    per_page = PAGE_SIZE // SEL_BLOCK
    csel = cache_sel.reshape(DEPTH, -1, per_page, SEL_BLOCK, hk, 2, HEAD_DIM)

    def sel_spec(kk):
        def imap(bi, g, pt, ix):
            blk = ix[(bi * hk + g) * N_OLD_SEL + kk]
            return (layer, pt[bi * N_PAGES + blk // per_page], blk % per_page, 0, g, 0, 0)
        return pl.BlockSpec((None, None, None, SEL_BLOCK, None, 2, HEAD_DIM), imap)

    bgj = lambda last: pl.BlockSpec((None, None, Q_PER_KV, last), lambda bi, g, pt, ix: (bi, g, 0, 0))
    new_spec = pl.BlockSpec((None, None, 1, 2 * HEAD_DIM), lambda bi, g, pt, ix: (bi, g, 0, 0))
    in_specs = [sel_spec(kk) for kk in range(N_OLD_SEL)] + [
        pl.BlockSpec((None, None, WINDOW, None, 2, HEAD_DIM), lambda bi, g, pt, ix: (layer, bi, 0, g, 0, 0)),
        new_spec, new_spec, bgj(HEAD_DIM), bgj(HEAD_DIM), bgj(3),
        pl.BlockSpec((None, N_BLK_S - 1, Q_PER_KV, SEL_BLOCK), lambda bi, g, pt, ix: (g, 0, 0, 0)),
        pl.BlockSpec((None, Q_PER_KV, WINDOW), lambda bi, g, pt, ix: (g, 0, 0)),
        pl.BlockSpec((None, Q_PER_KV, 1), lambda bi, g, pt, ix: (g, 0, 0))]
    gs = pltpu.PrefetchScalarGridSpec(
        num_scalar_prefetch=2, grid=(b, hk), in_specs=in_specs,
        out_specs=bgj(HEAD_DIM))
    return pl.pallas_call(
        _nsa_sample_attn_kernel, grid_spec=gs,
        out_shape=jax.ShapeDtypeStruct((b, hk, Q_PER_KV, HEAD_DIM), F32),
        compiler_params=_cp("parallel", "parallel"), name="nsa_sample_attn")(
            page_table.reshape(-1), idx.reshape(-1), *([csel] * N_OLD_SEL), cache_win,
            kvs_new.reshape(b, hk, 1, 2 * HEAD_DIM), kvw_new.reshape(b, hk, 1, 2 * HEAD_DIM),
            q_s.reshape(b, hk, Q_PER_KV, HEAD_DIM), o_c.reshape(b, hk, Q_PER_KV, HEAD_DIM), gn_s, bs, bw, b0)


def _router_kernel(h_ref, w_ref, b_ref, o_ref):
    logits = _dot(h_ref[...], w_ref[...]) + b_ref[...]
    lane = lax.broadcasted_iota(jnp.int32, logits.shape, 1)
    lanef = lane.astype(F32)

    def softmax(mask):
        s = jnp.where(mask, logits, NEG)
        e = jnp.where(mask, jnp.exp(s - jnp.max(s, axis=-1, keepdims=True)), 0.0)
        return e / jnp.sum(e, axis=-1, keepdims=True)

    def top1(p, mask):
        mx = jnp.max(jnp.where(mask, p, -1.0), axis=-1, keepdims=True)
        idx = jnp.min(jnp.where(mask & (p == mx), lanef, 1e9), axis=-1, keepdims=True)
        return mx, idx

    gmask = (lane >= N_EXPERTS) & (lane < N_EXPERTS + MOE_GROUPS)
    pg = softmax(gmask)
    p_top, g_idx = top1(pg, gmask)
    e_lo = (g_idx - N_EXPERTS) * MOE_PER_GROUP
    emask = (lanef >= e_lo) & (lanef < e_lo + MOE_PER_GROUP)
    pe = softmax(emask)
    w_a, i_a = top1(pe, emask)
    mask_b = emask & (lanef != i_a)
    w_b, i_b = top1(pe, mask_b)
    tot = w_a + w_b
    gate = jnp.where(lanef == i_a, w_a / tot * p_top, jnp.where(lanef == i_b, w_b / tot * p_top, 0.0))
    o_ref[...] = gate


def _router(hn, w_group, b_group, w_exp, b_exp):
    w = jnp.concatenate([w_exp, w_group, jnp.zeros((D_MODEL, 128 - N_EXPERTS - MOE_GROUPS), F32)], axis=1)
    bias = jnp.concatenate([b_exp, b_group, jnp.zeros((128 - N_EXPERTS - MOE_GROUPS,), F32)]).reshape(1, 128)
    return pl.pallas_call(
        _router_kernel, grid=(M_ALL // TM,),
        in_specs=[pl.BlockSpec((TM, D_MODEL), lambda i: (i, 0)),
                  pl.BlockSpec((D_MODEL, 128), lambda i: (0, 0)),
                  pl.BlockSpec((1, 128), lambda i: (0, 0))],
        out_specs=pl.BlockSpec((TM, 128), lambda i: (i, 0)),
        out_shape=jax.ShapeDtypeStruct((M_ALL, 128), F32),
        compiler_params=_cp("parallel"), name="moe_router")(hn, w.astype(BF16), bias)


def _moe_kernel(h_ref, win_ref, wout_ref, gate_ref, x_ref, o_ref, acc_scr):
    e = pl.program_id(1)

    @pl.when(e == 0)
    def _():
        acc_scr[...] = jnp.zeros(acc_scr.shape, F32)

    u = _dot(h_ref[...], win_ref[...])
    gate = gate_ref[...]
    lane = lax.broadcasted_iota(jnp.int32, gate.shape, 1)
    ge = jnp.sum(jnp.where(lane == e, gate, 0.0), axis=-1, keepdims=True)
    act = _silu(u[:, :D_EXPERT]) * u[:, D_EXPERT:] * ge
    acc_scr[...] = acc_scr[...] + _dot(act.astype(BF16), wout_ref[...])

    @pl.when(e == N_EXPERTS - 1)
    def _():
        o_ref[...] = x_ref[...] + acc_scr[...]


def _moe(hn, gate, x, w_in_e, w_out_e):
    return pl.pallas_call(
        _moe_kernel, grid=(M_ALL // TM, N_EXPERTS),
        in_specs=[pl.BlockSpec((TM, D_MODEL), lambda i, e: (i, 0)),
                  pl.BlockSpec((None, D_MODEL, 2 * D_EXPERT), lambda i, e: (e, 0, 0)),
                  pl.BlockSpec((None, D_EXPERT, D_MODEL), lambda i, e: (e, 0, 0)),
                  pl.BlockSpec((TM, 128), lambda i, e: (i, 0)),
                  pl.BlockSpec((TM, D_MODEL), lambda i, e: (i, 0))],
        out_specs=pl.BlockSpec((TM, D_MODEL), lambda i, e: (i, 0)),
        out_shape=jax.ShapeDtypeStruct((M_ALL, D_MODEL), F32),
        scratch_shapes=[pltpu.VMEM((TM, D_MODEL), F32)],
        compiler_params=_cp("parallel", "arbitrary"), name="moe_experts")(hn, w_in_e, w_out_e, gate, x)


def _prep_w_in(w):
    offs = np.cumsum((0,) + IN_SIZES)
    z, xbc, dt, q, kvc, kvs, kvw, gn, gm = [w[:, offs[k]:offs[k + 1]] for k in range(len(IN_SIZES))]
    small = jnp.concatenate([dt, gn, jnp.zeros((D_MODEL, 256 - dt.shape[1] - gn.shape[1]), w.dtype)], axis=1)
    return jnp.concatenate([z, xbc, q, kvc, kvs, kvw, gm, small], axis=1).astype(BF16)


def _layer(l, x, p, cache_cmp_kv, cache_sel_kv, cache_win_kv, state_ssm_l, state_conv_l, page_table, rel_bias):
    s = SEQ
    b = DEC_BATCH
    h1 = _rmsnorm(x, p['norm1_w'], BF16)
    proj = _mm(h1, _prep_w_in(p['w_in']), tn=1792, out_dtype=F32, name="in_proj")

    kv_new = {name: proj[:, c:c + KV_COLS] for name, c in (('cmp', C_KVC), ('sel', C_KVS), ('win', C_KVW))}

    act_p = _conv_prompt(proj, p['conv_w'], p['conv_b'])
    bt = act_p[:, D_SSD:D_SSD + SSD_GROUPS * SSD_STATE].T.astype(BF16)
    dtr = proj[:, C_SMALL:C_SMALL + SSD_HEADS]
    y_p, ssm_p = _ssd_scan_prompt(act_p, bt, dtr[:s], p['dt_bias'], p['a_log'], p['d_skip'])
    conv_p = proj[s - (SSD_CONV - 1):s, C_XBC:C_XBC + CONV_CH]

    xbc_s = proj[s:, C_XBC:C_XBC + CONV_CH]
    act_s = _conv_sample(state_conv_l.transpose(1, 0, 2), xbc_s, p['conv_w'], p['conv_b'])
    state_t = state_ssm_l.reshape(b, D_SSD, SSD_STATE).transpose(0, 2, 1)
    y_s, h_s = _ssd_step_sample(state_t, act_s, dtr[s:], p['dt_bias'], p['a_log'], p['d_skip'])
    ssm_s = h_s.transpose(0, 2, 1).reshape(b, SSD_HEADS, SSD_HEADDIM, SSD_STATE)
    conv_s = jnp.concatenate([state_conv_l[:, 1:], xbc_s[:, None, :]], axis=1)

    yn = _gated_norm(y_p, y_s.reshape(b, D_SSD), proj, p['ssd_norm_w'])
    y_a = _mm(yn, p['w_ssd_out'].astype(BF16), tn=1024, out_dtype=F32, name="ssd_out")

    b1c = _cmp_bias(p['cmp_pe'], p['cmp_w1'], p['cmp_b1'])
    w1c = (p['cmp_w1'].reshape(2, 2, CMP_STRIDE * HEAD_DIM, HEAD_DIM).transpose(0, 2, 1, 3)
           .reshape(2, CMP_STRIDE * HEAD_DIM, 2 * HEAD_DIM).astype(BF16))
    w2 = p['cmp_w2'].astype(BF16)

    n_grp_p = s // (N_PAGES * PAGE_SIZE)
    kvc_pages = kv_new['cmp'][:s].reshape(1, s // PAGE_SIZE, PAGE_SIZE, NSA_KV_HEADS, 2, HEAD_DIM)
    pt_p = jnp.arange(n_grp_p * N_PAGES, dtype=jnp.int32)
    pp = _chunk_proj(kvc_pages, 0, pt_p, w1c, n_grp_p)
    kcv_p = _cmp_mlp(pp, b1c, w2, 1, F32)[0]
    attn_p = _nsa_prompt(proj, kcv_p, kv_new['sel'][:s], kv_new['win'][:s], rel_bias, s)

    ps = _chunk_proj(cache_cmp_kv, l, page_table.reshape(-1), w1c, b)
    kcv_s = _cmp_mlp(ps, b1c, w2, b, BF16)
    q_s = proj[s:, C_Q:C_Q + NSA_HEADS * HEAD_DIM].reshape(b, NSA_HEADS, HEAD_DIM)
    o_c, idx = _nsa_sample_cmp(q_s, kcv_s, rel_bias)
    gn_s = (proj[s:, C_SMALL + GATE_LANE0:C_SMALL + GATE_LANE0 + 3 * NSA_HEADS]
            .reshape(b, 3, NSA_KV_HEADS, Q_PER_KV).transpose(0, 2, 3, 1))
    attn_s = _nsa_sample_attn(l, cache_sel_kv, cache_win_kv, page_table, idx[:, :N_OLD_SEL],
                              kv_new['sel'][s:], kv_new['win'][s:], q_s, o_c, gn_s, rel_bias)
    attn = jnp.concatenate([attn_p, attn_s.reshape(b, NSA_HEADS * HEAD_DIM).astype(BF16)], axis=0)

    tn = 1024
    m = _mm(attn, p['w_nsa_out'].astype(BF16), tn=tn, out_dtype=BF16, name="nsa_out_merge",
            extras=((y_a, 0), (proj, C_GM // tn), (proj, (C_GM + D_MODEL) // tn)), epilogue=_merge_epilogue)
    x = _mm(m, p['w_out'].astype(BF16), tn=tn, out_dtype=F32, name="out_proj", extras=((x, 0),),
            epilogue=_resid_epilogue)
    h2 = _rmsnorm(x, p['norm2_w'], BF16)
    gate = _router(h2, p['w_router_group'], p['b_router_group'], p['w_router_exp'], p['b_router_exp'])
    x = _moe(h2, gate, x, p['w_exp_in'].astype(BF16), p['w_exp_out'].astype(BF16))

    kv5 = lambda a: a.reshape(a.shape[0], NSA_KV_HEADS, 2, HEAD_DIM)
    win_s = jnp.concatenate([cache_win_kv[l][:, 1:], kv5(kv_new['win'][s:])[:, None]], axis=1)
    outs = dict(
        cmp_p=kv5(kv_new['cmp'][:s])[None], cmp_s=kv5(kv_new['cmp'][s:])[:, None],
        sel_p=kv5(kv_new['sel'][:s])[None], sel_s=kv5(kv_new['sel'][s:])[:, None],
        win_p=kv5(kv_new['win'][s - WINDOW:s])[None], win_s=win_s,
        ssm_p=ssm_p[None], ssm_s=ssm_s, conv_p=conv_p[None], conv_s=conv_s)
    return x, outs


def kernel(x_prompt, x_sample, cache_cmp_kv, cache_sel_kv, cache_win_kv, state_ssm, state_conv, page_table,
           rel_bias, norm1_w, norm2_w, final_norm_w, w_in, conv_w, conv_b, dt_bias, a_log, d_skip, ssd_norm_w,
           w_ssd_out, cmp_pe, cmp_w1, cmp_b1, cmp_w2, w_nsa_out, w_out, w_router_group, b_router_group,
           w_router_exp, b_router_exp, w_exp_in, w_exp_out):
    x = jnp.concatenate([x_prompt[0], x_sample[:, 0]], axis=0)
    per_layer = []
    for l in range(DEPTH):
        p = {'norm1_w': norm1_w[l], 'norm2_w': norm2_w[l], 'w_in': w_in[l], 'conv_w': conv_w[l],
             'conv_b': conv_b[l], 'dt_bias': dt_bias[l], 'a_log': a_log[l], 'd_skip': d_skip[l],
             'ssd_norm_w': ssd_norm_w[l], 'w_ssd_out': w_ssd_out[l], 'cmp_pe': cmp_pe[l], 'cmp_w1': cmp_w1[l],
             'cmp_b1': cmp_b1[l], 'cmp_w2': cmp_w2[l], 'w_nsa_out': w_nsa_out[l], 'w_out': w_out[l],
             'w_router_group': w_router_group[l], 'b_router_group': b_router_group[l],
             'w_router_exp': w_router_exp[l], 'b_router_exp': b_router_exp[l],
             'w_exp_in': w_exp_in[l], 'w_exp_out': w_exp_out[l]}
        x, outs = _layer(l, x, p, cache_cmp_kv, cache_sel_kv, cache_win_kv, state_ssm[l], state_conv[l],
                         page_table, rel_bias)
        per_layer.append(outs)
    y = _rmsnorm(x, final_norm_w, F32)
    stack = lambda k: jnp.stack([o[k] for o in per_layer])
    return (y[:SEQ][None], y[SEQ:][:, None], stack('cmp_p'), stack('cmp_s'), stack('sel_p'), stack('sel_s'),
            stack('win_p'), stack('win_s'), stack('ssm_p'), stack('ssm_s'), stack('conv_p'), stack('conv_s'))
```

```python
import functools
import math

import numpy as np
import jax
import jax.numpy as jnp
from jax import lax
from jax.experimental import pallas as pl
from jax.experimental.pallas import tpu as pltpu

F32 = jnp.float32
BF16 = jnp.bfloat16
NEG = -1e30

D_MODEL = 2048
SEQ = 8192
DEPTH = 2
DEC_BATCH = 128
PAST_LEN = 2048
PAGE_SIZE = 128
N_PAGES = PAST_LEN // PAGE_SIZE
D_SSD = 4096
SSD_HEADDIM = 64
SSD_HEADS = 64
SSD_GROUPS = 8
SSD_STATE = 128
SSD_CONV = 4
SSD_CHUNK = 256
CONV_CH = 6144
NSA_HEADS = 16
NSA_KV_HEADS = 4
HEAD_DIM = 128
Q_PER_KV = 4
CMP_STRIDE = 16
SEL_BLOCK = 64
N_SEL = 16
WINDOW = 512
Q_BLOCK = 128
FORCED_SCORE = 1e4
REL_BUCKETS = 32
REL_MAX_DIST = 128
MOE_GROUPS = 4
MOE_PER_GROUP = 4
N_EXPERTS = 16
D_EXPERT = 512
RMS_EPS = 1e-6
KV_COLS = 1024
IN_SIZES = (D_SSD, CONV_CH, SSD_HEADS, NSA_HEADS * HEAD_DIM, KV_COLS, KV_COLS, KV_COLS, 3 * NSA_HEADS, 2 * D_MODEL)
SCALE = HEAD_DIM ** -0.5

C_Z = 0
C_XBC = 4096
C_Q = 10240
C_KVC = 12288
C_KVS = 13312
C_KVW = 14336
C_GM = 15360
C_SMALL = 19456
N_PROJ = 19712
GATE_LANE0 = 64

M_ALL = SEQ + DEC_BATCH
TM = 640
VMEM_LIMIT_MIB = 56


def _cp(*sem):
    return pltpu.CompilerParams(dimension_semantics=sem, vmem_limit_bytes=VMEM_LIMIT_MIB * 1024 * 1024)


def _dot(a, b):
    return jnp.dot(a, b, preferred_element_type=F32)


def _dot_nt(a, b):
    return lax.dot_general(a, b, (((1,), (1,)), ((), ())), preferred_element_type=F32)


def _dot_f32(a, b):
    return jnp.dot(a, b, preferred_element_type=F32, precision=lax.Precision.HIGHEST)


def _silu(x):
    return x * jax.nn.sigmoid(x)


def _softplus(x):
    return jnp.maximum(x, 0.0) + jnp.log1p(jnp.exp(-jnp.abs(x)))


def _split_hi_lo(v):
    hi = v.astype(BF16)
    lo = (v - hi.astype(F32)).astype(BF16)
    return hi, lo


def _expand(v, e):
    hi, lo = _split_hi_lo(v)
    return _dot(hi, e) + _dot(lo, e)


def _rmsnorm_kernel(x_ref, w_ref, o_ref):
    x = x_ref[...]
    ms = jnp.mean(x * x, axis=-1, keepdims=True)
    o_ref[...] = (x * lax.rsqrt(ms + RMS_EPS) * w_ref[...]).astype(o_ref.dtype)


def _rmsnorm(x, w, out_dtype, tm=TM):
    m, d = x.shape
    return pl.pallas_call(
        _rmsnorm_kernel, grid=(m // tm,),
        in_specs=[pl.BlockSpec((tm, d), lambda i: (i, 0)), pl.BlockSpec((1, d), lambda i: (0, 0))],
        out_specs=pl.BlockSpec((tm, d), lambda i: (i, 0)),
        out_shape=jax.ShapeDtypeStruct((m, d), out_dtype),
        compiler_params=_cp("parallel"), name="rmsnorm")(x, w.reshape(1, d))


def _gated_norm_kernel(yp_ref, ys_ref, z_ref, w_ref, o_ref, *, n_prompt_tiles):
    i = pl.program_id(0)
    y = jnp.where(i < n_prompt_tiles, yp_ref[...], ys_ref[...])
    gated = y * _silu(z_ref[...])
    ms = jnp.mean(gated * gated, axis=-1, keepdims=True)
    o_ref[...] = (gated * lax.rsqrt(ms + RMS_EPS) * w_ref[...]).astype(o_ref.dtype)


def _gated_norm(y_p, y_s, proj, w):
    tm = DEC_BATCH
    npt = SEQ // tm
    return pl.pallas_call(
        functools.partial(_gated_norm_kernel, n_prompt_tiles=npt), grid=(M_ALL // tm,),
        in_specs=[pl.BlockSpec((tm, D_SSD), lambda i: (jnp.minimum(i, npt - 1), 0)),
                  pl.BlockSpec((tm, D_SSD), lambda i: (0, 0)),
                  pl.BlockSpec((tm, D_SSD), lambda i: (i, C_Z // D_SSD)),
                  pl.BlockSpec((1, D_SSD), lambda i: (0, 0))],
        out_specs=pl.BlockSpec((tm, D_SSD), lambda i: (i, 0)),
        out_shape=jax.ShapeDtypeStruct((M_ALL, D_SSD), BF16),
        compiler_params=_cp("parallel"), name="gated_norm")(y_p, y_s, proj, w.reshape(1, D_SSD))


def _mm_kernel(a_ref, w_ref, *rest, epilogue):
    o_ref = rest[-1]
    acc = _dot(a_ref[...], w_ref[...])
    if epilogue is not None:
        acc = epilogue(acc, *[r[...] for r in rest[:-1]])
    o_ref[...] = acc.astype(o_ref.dtype)


def _mm(a, w, *, tn, out_dtype, name, extras=(), epilogue=None, tm=TM):
    m, k = a.shape
    n = w.shape[1]
    in_specs = [pl.BlockSpec((tm, k), lambda j, i: (i, 0)), pl.BlockSpec((k, tn), lambda j, i: (0, j))]
    args = [a, w]
    for arr, coff in extras:
        in_specs.append(pl.BlockSpec((tm, tn), lambda j, i, coff=coff: (i, coff + j)))
        args.append(arr)
    return pl.pallas_call(
        functools.partial(_mm_kernel, epilogue=epilogue), grid=(n // tn, m // tm),
        in_specs=in_specs, out_specs=pl.BlockSpec((tm, tn), lambda j, i: (i, j)),
        out_shape=jax.ShapeDtypeStruct((m, n), out_dtype),
        compiler_params=_cp("parallel", "parallel"), name=name)(*args)


def _merge_epilogue(y_b, y_a, g_a, g_b):
    return jax.nn.sigmoid(g_a) * y_a + jax.nn.sigmoid(g_b) * y_b


def _resid_epilogue(acc, x):
    return x + acc


CONV_TR = 512
CONV_TC = 512


def _conv_prompt_kernel(x_ref, halo_ref, w_ref, b_ref, o_ref):
    i = pl.program_id(1)
    halo = jnp.where(i > 0, halo_ref[...], 0.0)
    x = jnp.concatenate([halo, x_ref[...]], axis=0)
    n = x.shape[0]
    acc = b_ref[...] + x[8:] * w_ref[SSD_CONV - 1:SSD_CONV, :]
    for k in range(SSD_CONV - 1):
        shifted = pltpu.roll(x, SSD_CONV - 1 - k, axis=0)
        acc = acc + shifted[8:] * w_ref[k:k + 1, :]
    del n
    o_ref[...] = _silu(acc)


def _conv_prompt(proj, conv_w, conv_b):
    nct = CONV_CH // CONV_TC
    nrt = SEQ // CONV_TR
    c0 = C_XBC // CONV_TC
    return pl.pallas_call(
        _conv_prompt_kernel, grid=(nct, nrt),
        in_specs=[pl.BlockSpec((CONV_TR, CONV_TC), lambda c, i: (i, c0 + c)),
                  pl.BlockSpec((8, CONV_TC), lambda c, i: (jnp.maximum(i * (CONV_TR // 8) - 1, 0), c0 + c)),
                  pl.BlockSpec((SSD_CONV, CONV_TC), lambda c, i: (0, c)),
                  pl.BlockSpec((1, CONV_TC), lambda c, i: (0, c))],
        out_specs=pl.BlockSpec((CONV_TR, CONV_TC), lambda c, i: (i, c)),
        out_shape=jax.ShapeDtypeStruct((SEQ, CONV_CH), F32),
        compiler_params=_cp("parallel", "parallel"), name="conv_prompt")(
            proj, proj, conv_w, conv_b.reshape(1, CONV_CH))


GW = SSD_HEADS // SSD_GROUPS * SSD_HEADDIM
HPG = SSD_HEADS // SSD_GROUPS


def _scan_kernel(xs_ref, b_ref, c_ref, bt_ref, dtr_ref, dtrt_ref, dtb_r_ref, dtb_c_ref, al_r_ref, al_c_ref,
                 dskip_ref, e_ref, y_ref, hout_ref, h_scr):
    c = pl.program_id(1)
    q = SSD_CHUNK

    @pl.when(c == 0)
    def _():
        h_scr[...] = jnp.zeros((SSD_STATE, GW), F32)

    dt = _softplus(dtr_ref[...] + dtb_r_ref[...])
    d_a = dt * (-jnp.exp(al_r_ref[...]))
    dt_t = _softplus(dtrt_ref[...] + dtb_c_ref[...])
    d_a_t = dt_t * (-jnp.exp(al_c_ref[...]))
    row = lax.broadcasted_iota(jnp.int32, (q, q), 0)
    col = lax.broadcasted_iota(jnp.int32, (q, q), 1)
    tril = row >= col
    cum = _dot_f32(tril.astype(F32), d_a)
    cum_t = _dot_f32(d_a_t, (col >= row).astype(F32))
    cum_last = cum[q - 1:q, :]
    e = e_ref[...]
    dt_e = _expand(dt, e)
    ecum_e = _expand(jnp.exp(cum), e)
    tail_e = _expand(jnp.exp(cum_last - cum), e)
    elast_e = _expand(jnp.broadcast_to(jnp.exp(cum_last), (8, 128)), e)[0:1, :]

    xs = xs_ref[...]
    xdt = xs * dt_e
    xdt_b = xdt.astype(BF16)
    xw = (xdt * tail_e).astype(BF16)
    bg = b_ref[...].astype(BF16)
    cg = c_ref[...].astype(BF16)
    cb = _dot_nt(cg, bg)
    h_prev = h_scr[...]
    y = _dot(cg, h_prev.astype(BF16)) * ecum_e
    lane = lax.broadcasted_iota(jnp.int32, (q, 128), 1)
    ys = []
    for pair in range(HPG // 2):
        xpair = xdt_b[:, pair * 128:(pair + 1) * 128]
        top = jnp.where(lane < SSD_HEADDIM, xpair, jnp.zeros_like(xpair))
        bot = jnp.where(lane >= SSD_HEADDIM, xpair, jnp.zeros_like(xpair))
        atts = []
        for hh in range(2):
            h8 = pair * 2 + hh
            seg = jnp.exp(jnp.where(tril, cum[:, h8:h8 + 1] - cum_t[h8:h8 + 1, :], NEG))
            atts.append((cb * seg).astype(BF16))
        ys.append(_dot(jnp.concatenate(atts, axis=1), jnp.concatenate([top, bot], axis=0)))
    y = y + jnp.concatenate(ys, axis=1) + dskip_ref[...] * xs
    y_ref[...] = y
    h_new = h_prev * elast_e + _dot(bt_ref[...], xw)
    h_scr[...] = h_new

    @pl.when(c == pl.num_programs(1) - 1)
    def _():
        hout_ref[...] = h_new


def _head_expand_matrix(n_heads, rows):
    e = np.zeros((rows, n_heads * SSD_HEADDIM), np.float32)
    for h in range(n_heads):
        e[h, h * SSD_HEADDIM:(h + 1) * SSD_HEADDIM] = 1.0
    return jnp.asarray(e, BF16)


def _ssd_scan_prompt(act, bt, dtr, dt_bias, a_log, d_skip):
    q = SSD_CHUNK
    nc = SEQ // q
    dtr_g = dtr.reshape(SEQ, SSD_GROUPS, HPG).transpose(1, 0, 2)
    dtr_g128 = jnp.pad(dtr_g, ((0, 0), (0, 0), (0, 128 - HPG)))
    dtrt_g = dtr_g.transpose(0, 2, 1)
    pad_r = lambda v: jnp.pad(v.reshape(SSD_GROUPS, 1, HPG), ((0, 0), (0, 0), (0, 128 - HPG)))
    col_c = lambda v: v.reshape(SSD_GROUPS, HPG, 1)
    dskip_e = jnp.repeat(d_skip, SSD_HEADDIM).reshape(1, D_SSD)
    e = _head_expand_matrix(HPG, 128)
    small = lambda shape: pl.BlockSpec((None,) + shape, lambda g, c: (g, 0, 0))
    y, h_out = pl.pallas_call(
        _scan_kernel, grid=(SSD_GROUPS, nc),
        in_specs=[pl.BlockSpec((q, GW), lambda g, c: (c, g)),
                  pl.BlockSpec((q, SSD_STATE), lambda g, c: (c, D_SSD // SSD_STATE + g)),
                  pl.BlockSpec((q, SSD_STATE), lambda g, c: (c, (D_SSD + SSD_GROUPS * SSD_STATE) // SSD_STATE + g)),
                  pl.BlockSpec((SSD_STATE, q), lambda g, c: (g, c)),
                  pl.BlockSpec((None, q, 128), lambda g, c: (g, c, 0)),
                  pl.BlockSpec((None, HPG, q), lambda g, c: (g, 0, c)),
                  small((1, 128)), small((HPG, 1)), small((1, 128)), small((HPG, 1)),
                  pl.BlockSpec((1, GW), lambda g, c: (0, g)),
                  pl.BlockSpec((128, GW), lambda g, c: (0, 0))],
        out_specs=[pl.BlockSpec((q, GW), lambda g, c: (c, g)),
                   pl.BlockSpec((None, SSD_STATE, GW), lambda g, c: (g, 0, 0))],
        out_shape=[jax.ShapeDtypeStruct((SEQ, D_SSD), F32),
                   jax.ShapeDtypeStruct((SSD_GROUPS, SSD_STATE, GW), F32)],
        scratch_shapes=[pltpu.VMEM((SSD_STATE, GW), F32)],
        compiler_params=_cp("parallel", "arbitrary"), name="ssd_scan")(
            act, act, act, bt, dtr_g128, dtrt_g, pad_r(dt_bias), col_c(dt_bias), pad_r(a_log), col_c(a_log),
            dskip_e, e)
    h_last = h_out.reshape(SSD_GROUPS, SSD_STATE, HPG, SSD_HEADDIM).transpose(0, 2, 3, 1)
    return y, h_last.reshape(SSD_HEADS, SSD_HEADDIM, SSD_STATE)


def _conv_sample_kernel(s_ref, x_ref, w_ref, b_ref, o_ref):
    acc = b_ref[...] + x_ref[...] * w_ref[SSD_CONV - 1:SSD_CONV, :]
    for k in range(SSD_CONV - 1):
        acc = acc + s_ref[k] * w_ref[k:k + 1, :]
    o_ref[...] = _silu(acc)


def _conv_sample(state_t, xbc, conv_w, conv_b):
    tc = 1024
    return pl.pallas_call(
        _conv_sample_kernel, grid=(CONV_CH // tc,),
        in_specs=[pl.BlockSpec((SSD_CONV - 1, DEC_BATCH, tc), lambda c: (0, 0, c)),
                  pl.BlockSpec((DEC_BATCH, tc), lambda c: (0, c)),
                  pl.BlockSpec((SSD_CONV, tc), lambda c: (0, c)),
                  pl.BlockSpec((1, tc), lambda c: (0, c))],
        out_specs=pl.BlockSpec((DEC_BATCH, tc), lambda c: (0, c)),
        out_shape=jax.ShapeDtypeStruct((DEC_BATCH, CONV_CH), F32),
        compiler_params=_cp("parallel"), name="conv_sample")(state_t, xbc, conv_w, conv_b.reshape(1, CONV_CH))


def _ssd_step_kernel(h_ref, xs_ref, dtr_ref, bn_ref, cn_ref, dtb_ref, al_ref, dskip_ref, e64_ref, e8_ref,
                     y_ref, hout_ref):
    dt = _softplus(dtr_ref[...] + dtb_ref[...])
    dec = jnp.exp(dt * (-jnp.exp(al_ref[...])))
    e64 = e64_ref[...]
    dt_e = _expand(dt, e64)[0:1, :]
    dec_e = _expand(dec, e64)[0:1, :]
    xs = xs_ref[...]
    xdt = xs * dt_e
    e8 = e8_ref[...]
    b_exp = _dot(bn_ref[...].astype(BF16), e8)
    c_exp = _dot(cn_ref[...].astype(BF16), e8)
    h_new = h_ref[...] * dec_e + b_exp * xdt
    hout_ref[...] = h_new
    y_ref[...] = jnp.sum(h_new * c_exp, axis=0, keepdims=True) + dskip_ref[...] * xs


def _ssd_step_sample(state_t, act_s, dtr_s, dt_bias, a_log, d_skip):
    b = DEC_BATCH
    xs3 = act_s[:, :D_SSD].reshape(b, 1, D_SSD)
    bn = act_s[:, D_SSD:D_SSD + 1024].reshape(b, SSD_GROUPS, SSD_STATE).transpose(0, 2, 1)
    cn = act_s[:, D_SSD + 1024:].reshape(b, SSD_GROUPS, SSD_STATE).transpose(0, 2, 1)
    bn = jnp.pad(bn, ((0, 0), (0, 0), (0, 128 - SSD_GROUPS)))
    cn = jnp.pad(cn, ((0, 0), (0, 0), (0, 128 - SSD_GROUPS)))
    dtr8 = jnp.pad(dtr_s.reshape(b, 1, SSD_HEADS), ((0, 0), (0, 7), (0, 128 - SSD_HEADS)))
    row128 = lambda v: jnp.pad(v.reshape(1, SSD_HEADS), ((0, 0), (0, 128 - SSD_HEADS)))
    dskip_e = jnp.repeat(d_skip, SSD_HEADDIM).reshape(1, D_SSD)
    e64 = _head_expand_matrix(SSD_HEADS, 128)
    e8np = np.zeros((128, D_SSD), np.float32)
    for g in range(SSD_GROUPS):
        e8np[g, g * GW:(g + 1) * GW] = 1.0
    e8 = jnp.asarray(e8np, BF16)
    full = lambda shape: pl.BlockSpec(shape, lambda i: (0,) * len(shape))
    per_b = lambda shape: pl.BlockSpec((None,) + shape, lambda i: (i, 0, 0))
    return pl.pallas_call(
        _ssd_step_kernel, grid=(b,),
        in_specs=[per_b((SSD_STATE, D_SSD)), per_b((1, D_SSD)), per_b((8, 128)),
                  per_b((SSD_STATE, 128)), per_b((SSD_STATE, 128)),
                  full((1, 128)), full((1, 128)), full((1, D_SSD)), full((128, D_SSD)), full((128, D_SSD))],
        out_specs=[per_b((1, D_SSD)), per_b((SSD_STATE, D_SSD))],
        out_shape=[jax.ShapeDtypeStruct((b, 1, D_SSD), F32), jax.ShapeDtypeStruct((b, SSD_STATE, D_SSD), F32)],
        compiler_params=_cp("parallel"), name="ssd_step")(
            state_t, xs3, dtr8, bn, cn, row128(dt_bias), row128(a_log), dskip_e, e64, e8)


def _chunk_proj_kernel(pt_ref, *refs):
    del pt_ref
    pages = refs[:N_PAGES]
    w_ref, o_ref = refs[N_PAGES:]
    nch = N_PAGES * PAGE_SIZE // CMP_STRIDE
    cpp = PAGE_SIZE // CMP_STRIDE
    for s in range(2):
        lhs_h = []
        for h in range(NSA_KV_HEADS):
            per_r = []
            for r in range(CMP_STRIDE):
                rows = [pg[pl.ds(r, cpp, stride=CMP_STRIDE), h, s, :] for pg in pages]
                per_r.append(jnp.concatenate(rows, axis=0).astype(BF16))
            lhs_h.append(jnp.concatenate(per_r, axis=1))
        res = _dot(jnp.concatenate(lhs_h, axis=0), w_ref[s])
        for h in range(NSA_KV_HEADS):
            o_ref[s, h] = res[h * nch:(h + 1) * nch]


def _chunk_proj(pages_arr, layer, page_table, w1c, n_groups):
    nch = N_PAGES * PAGE_SIZE // CMP_STRIDE

    def page_spec(pg):
        return pl.BlockSpec((None, None, PAGE_SIZE, NSA_KV_HEADS, 2, HEAD_DIM),
                            lambda b, pt: (layer, pt[b * N_PAGES + pg], 0, 0, 0, 0))

    in_specs = [page_spec(p) for p in range(N_PAGES)]
    in_specs.append(pl.BlockSpec((2, CMP_STRIDE * HEAD_DIM, 2 * HEAD_DIM), lambda b, pt: (0, 0, 0)))
    gs = pltpu.PrefetchScalarGridSpec(
        num_scalar_prefetch=1, grid=(n_groups,), in_specs=in_specs,
        out_specs=pl.BlockSpec((2, NSA_KV_HEADS, nch, 2 * HEAD_DIM), lambda b, pt: (0, 0, b, 0)))
    return pl.pallas_call(
        _chunk_proj_kernel, grid_spec=gs,
        out_shape=jax.ShapeDtypeStruct((2, NSA_KV_HEADS, n_groups * nch, 2 * HEAD_DIM), F32),
        compiler_params=_cp("parallel"), name="cmp_chunk_proj")(page_table, *([pages_arr] * N_PAGES), w1c)


def _cmp_bias_kernel(pe_ref, w_ref, b_ref, o_ref):
    o_ref[...] = b_ref[...] + _dot(pe_ref[...], w_ref[...])


def _cmp_bias(cmp_pe, cmp_w1, cmp_b1):
    k = 2 * CMP_STRIDE * HEAD_DIM
    pe = jnp.broadcast_to(cmp_pe.reshape(2, 1, k), (2, 8, k)).astype(BF16)
    w = cmp_w1.reshape(2, k, HEAD_DIM).astype(BF16)
    b = jnp.broadcast_to(cmp_b1.reshape(2, 1, HEAD_DIM), (2, 8, HEAD_DIM))
    return pl.pallas_call(
        _cmp_bias_kernel, grid=(2,),
        in_specs=[pl.BlockSpec((None, 8, k), lambda s: (s, 0, 0)),
                  pl.BlockSpec((None, k, HEAD_DIM), lambda s: (s, 0, 0)),
                  pl.BlockSpec((None, 8, HEAD_DIM), lambda s: (s, 0, 0))],
        out_specs=pl.BlockSpec((None, 8, HEAD_DIM), lambda s: (s, 0, 0)),
        out_shape=jax.ShapeDtypeStruct((2, 8, HEAD_DIM), F32),
        compiler_params=_cp("parallel"), name="cmp_bias")(pe, w, b)


def _cmp_mlp_kernel(p_ref, b_ref, w2_ref, o_ref):
    n = p_ref.shape[2]
    for s in range(2):
        for h in range(NSA_KV_HEADS):
            p = p_ref[s, h]
            nxt = pltpu.roll(p[:, HEAD_DIM:], n - 1, axis=0)
            hid = _silu(p[:, :HEAD_DIM] + nxt + b_ref[s, 0:1, :])
            o_ref[h * 2 + s] = _dot(hid.astype(BF16), w2_ref[s]).astype(o_ref.dtype)


def _cmp_mlp(p, b1c, w2, n_groups, out_dtype):
    nch = p.shape[2] // n_groups
    return pl.pallas_call(
        _cmp_mlp_kernel, grid=(n_groups,),
        in_specs=[pl.BlockSpec((2, NSA_KV_HEADS, nch, 2 * HEAD_DIM), lambda b: (0, 0, b, 0)),
                  pl.BlockSpec((2, 8, HEAD_DIM), lambda b: (0, 0, 0)),
                  pl.BlockSpec((2, HEAD_DIM, HEAD_DIM), lambda b: (0, 0, 0))],
        out_specs=pl.BlockSpec((None, 2 * NSA_KV_HEADS, nch, HEAD_DIM), lambda b: (b, 0, 0, 0)),
        out_shape=jax.ShapeDtypeStruct((n_groups, 2 * NSA_KV_HEADS, nch, HEAD_DIM), out_dtype),
        compiler_params=_cp("parallel"), name="cmp_mlp")(p, b1c, w2)


def _bucket_np(dist):
    n = np.maximum(dist, 0)
    exact = REL_BUCKETS // 2
    nf = np.maximum(n, 1).astype(np.float32)
    big = exact + (np.log(nf / np.float32(exact)) / np.float32(math.log(REL_MAX_DIST / exact))
                   * np.float32(REL_BUCKETS - exact)).astype(np.int32)
    return np.where(n < exact, n, np.minimum(big, REL_BUCKETS - 1))


def _bias_table(rel_bias_t, dist, valid):
    tab = rel_bias_t[:, _bucket_np(dist)]
    return jnp.where(jnp.asarray(valid), tab, NEG)


CMP_PAD = 120
CMP_ROWS = 768


def _wsel_np(n_cmp, n_blocks, rows, pad):
    w = np.zeros((rows, 128), np.float32)
    c0 = np.arange(n_cmp)[:, None] * CMP_STRIDE
    s0 = np.arange(n_blocks)[None, :] * SEL_BLOCK
    inter = np.minimum(c0 + 2 * CMP_STRIDE, s0 + SEL_BLOCK) - np.maximum(c0, s0)
    w[pad:pad + n_cmp, :n_blocks] = np.maximum(inter, 0).astype(np.float32) / (2 * CMP_STRIDE)
    return w


M0 = -1e20


def _nsa_prompt_kernel(q_ref, kcw_ref, vcw_ref, wselw_ref, kco_ref, vco_ref, wselo_ref,
                       ks_ref, vs_ref, kw_ref, vw_ref, small_ref, tc_ref, td_ref, tp_ref, cb_ref,
                       o_ref, m_scr, l_scr, acc_scr, imp_scr, out_scr, sel_scr, gate_scr):
    g = pl.program_id(0)
    i = pl.program_id(1)
    qb = Q_BLOCK
    w = Q_PER_KV * qb
    qf = q_ref[...]
    qt = jnp.concatenate([qf[:, j * HEAD_DIM:(j + 1) * HEAD_DIM].T.astype(BF16) for j in range(Q_PER_KV)],
                         axis=1)
    row = lax.broadcasted_iota(jnp.int32, (qb, qb), 0)
    col = lax.broadcasted_iota(jnp.int32, (qb, qb), 1)
    row4 = lax.broadcasted_iota(jnp.int32, (qb, w), 0)
    cb = cb_ref[...]
    tile4 = lambda x: jnp.concatenate([x] * Q_PER_KV, axis=1)

    gate_scr[...] = jax.nn.sigmoid(small_ref[...]).T

    def gate_row(branch):
        start = GATE_LANE0 + branch * NSA_HEADS + g * Q_PER_KV
        r8 = gate_scr[pl.ds(pl.multiple_of((start // 8) * 8, 8), 8), :]
        r4 = jnp.where(start % 8 == 0, r8[0:Q_PER_KV], r8[Q_PER_KV:2 * Q_PER_KV])
        return jnp.concatenate([r4[j:j + 1, :] for j in range(Q_PER_KV)], axis=1)

    def reset():
        m_scr[...] = jnp.full(m_scr.shape, M0, F32)
        l_scr[...] = jnp.zeros(l_scr.shape, F32)
        acc_scr[...] = jnp.zeros(acc_scr.shape, F32)

    def flash(parts, wsel_t=None):
        ss = []
        for k, _, bias, mask in parts:
            s = _dot(k, qt) * SCALE + bias
            ss.append(s if mask is None else jnp.where(mask, s, NEG))
        m_prev = m_scr[...]
        m_new = m_prev
        for s in ss:
            m_new = jnp.maximum(m_new, jnp.max(s, axis=0, keepdims=True))
        alpha = jnp.exp(m_prev - m_new)
        ps = [jnp.exp(s - m_new) for s in ss]
        lsum = jnp.sum(ps[0], axis=0, keepdims=True)
        for p in ps[1:]:
            lsum = lsum + jnp.sum(p, axis=0, keepdims=True)
        l_scr[...] = alpha * l_scr[...] + lsum
        m_scr[...] = m_new
        pb = jnp.concatenate([p.astype(BF16) for p in ps], axis=0)
        vt = jnp.concatenate([part[1] for part in parts], axis=1)
        acc_scr[...] = acc_scr[...] * alpha + _dot(vt, pb)
        if wsel_t is not None:
            imp_scr[...] = imp_scr[...] * alpha + _dot(wsel_t, pb)

    def normalized(ref):
        return ref[...] / jnp.maximum(l_scr[...], 1e-30)

    def lane_cat(ref, n):
        return jnp.concatenate([ref[u] for u in range(n)], axis=1)

    reset()
    imp_scr[...] = jnp.zeros(imp_scr.shape, F32)
    per_q = qb // CMP_STRIDE
    npos = lax.broadcasted_iota(jnp.int32, (4 * qb, w), 0)
    flash([(kco_ref[...].reshape(4 * qb, HEAD_DIM), lane_cat(vco_ref, 4), cb,
            (npos >= CMP_PAD) & (npos < i * per_q)),
           (kcw_ref[...], vcw_ref[...], tc_ref[...], (row4 + i * per_q) >= CMP_PAD)],
          jnp.concatenate([lane_cat(wselo_ref, 4), wselw_ref[...]], axis=1))
    out_scr[...] = gate_row(0) * normalized(acc_scr)
    impn = normalized(imp_scr)
    imp = impn[:, 0:qb]
    for j in range(1, Q_PER_KV):
        imp = imp + impn[:, j * qb:(j + 1) * qb]

    rowf = row.astype(F32)
    cur = 2 * i + (col >= SEL_BLOCK).astype(jnp.int32)
    forced = (row == 0) | (row == cur) | (row == cur - 1)
    score = jnp.where(forced, FORCED_SCORE, jnp.where(row <= cur, imp, -1.0))
    sel = jnp.zeros((qb, qb), F32)
    for _ in range(N_SEL):
        mx = jnp.max(score, axis=0, keepdims=True)
        idx = jnp.min(jnp.where(score == mx, rowf, 1e9), axis=0, keepdims=True)
        hit = rowf == idx
        sel = jnp.where(hit, 1.0, sel)
        score = jnp.where(hit, -3e38, score)
    sel_scr[...] = sel

    def block_rows(r, n_blocks):
        rows = [jnp.broadcast_to(r[u:u + 1, :], (SEL_BLOCK, qb)) for u in range(n_blocks)]
        return tile4(jnp.concatenate(rows, axis=0)) > 0.5

    reset()
    gt = 4
    n_full = jnp.maximum(i - 1, 0) // gt

    def sel_body(gi, carry):
        t0 = gi * gt
        r8 = sel_scr[pl.ds(pl.multiple_of(gi * 2 * gt, 8), 2 * gt), :]
        k = ks_ref[pl.ds(t0, gt)].reshape(gt * qb, HEAD_DIM)
        vt = jnp.concatenate([vs_ref[t0 + u] for u in range(gt)], axis=1)
        flash([(k, vt, cb, block_rows(r8, 2 * gt))])
        return carry

    lax.fori_loop(0, n_full, sel_body, 0)

    t0 = n_full * gt
    last = pl.num_programs(1) - 1
    r16 = jnp.concatenate(
        [sel_scr[pl.ds(pl.multiple_of(t0 * 2, 8), 8), :],
         sel_scr[pl.ds(pl.multiple_of(jnp.minimum(t0 * 2 + 8, qb - 8), 8), 8), :]], axis=0)
    tdv = td_ref[...]
    tpv = tp_ref[...]
    parts = []
    for u in range(gt + 1):
        kt = t0 + u
        ktc = jnp.minimum(kt, last)
        bias = jnp.where(kt == i, tdv, jnp.where(kt == i - 1, tpv, cb))
        mask = block_rows(r16[2 * u:2 * u + 2], 2) & (kt <= i)
        parts.append((ks_ref[ktc], vs_ref[ktc], bias, mask))
    flash(parts)
    out_scr[...] = out_scr[...] + gate_row(1) * normalized(acc_scr)

    reset()
    upper = tile4((row > col).astype(F32)) > 0.5
    n_back = WINDOW // qb
    parts = []
    for u in range(n_back + 1):
        kt = i - n_back + u
        ktc = jnp.maximum(kt, 0)
        bias = tdv if u == n_back else (tpv if u == n_back - 1 else cb)
        mask = (upper & (kt >= 0)) if u == 0 else jnp.broadcast_to(kt >= 0, (qb, w))
        parts.append((kw_ref[ktc], vw_ref[ktc], bias, mask))
    flash(parts)
    out = out_scr[...] + gate_row(2) * normalized(acc_scr)
    for j in range(Q_PER_KV):
        o_ref[:, j * HEAD_DIM:(j + 1) * HEAD_DIM] = out[:, j * qb:(j + 1) * qb].T.astype(o_ref.dtype)


def _nsa_prompt(proj, kcv, kv_sel, kv_win, rel_bias, seq):
    qb = Q_BLOCK
    nb = seq // qb
    hk = NSA_KV_HEADS
    w = Q_PER_KV * qb

    def key_tiles(x):
        x5 = x.astype(BF16).reshape(nb, qb, hk, 2, HEAD_DIM)
        return x5[:, :, :, 0, :].transpose(2, 0, 1, 3), x5[:, :, :, 1, :].transpose(2, 0, 3, 1)

    ks, vs = key_tiles(kv_sel)
    kw, vw = key_tiles(kv_win)

    nck = kcv.shape[1]
    padded = jnp.pad(kcv.astype(BF16), ((0, 0), (CMP_PAD, CMP_ROWS - CMP_PAD - nck), (0, 0)))
    kc_pad, vc_pad = padded[0::2], padded[1::2]
    n_old = 4 * qb
    kco = kc_pad[:, :n_old].reshape(hk, 4, qb, HEAD_DIM)
    vco = vc_pad[:, :n_old].reshape(hk, 4, qb, HEAD_DIM).transpose(0, 1, 3, 2)
    widx = (qb // CMP_STRIDE) * np.arange(nb)[:, None] + np.arange(qb)[None, :]
    kcw = kc_pad[:, widx]
    vcw = vc_pad[:, widx].transpose(0, 1, 3, 2)
    wsel = _wsel_np(seq // CMP_STRIDE - 1, seq // SEL_BLOCK, CMP_ROWS, CMP_PAD)
    wselo = jnp.asarray(wsel[:n_old].reshape(4, qb, 128).transpose(0, 2, 1), BF16)
    wselw = jnp.asarray(wsel[widx].transpose(0, 2, 1), BF16)

    rbt = rel_bias.T
    a = np.arange(qb)[:, None]
    b = np.arange(qb)[None, :]
    lanes = lambda t16: t16.reshape(hk, Q_PER_KV, qb, qb).transpose(0, 3, 1, 2).reshape(hk, qb, w)
    td = lanes(_bias_table(rbt, a - b, a >= b))
    tp = lanes(_bias_table(rbt, a - b + qb, np.ones((qb, qb), bool)))
    dist_c = a - CMP_STRIDE * b + (CMP_STRIDE * CMP_PAD - 2 * CMP_STRIDE + 1)
    tc = lanes(_bias_table(rbt, dist_c, dist_c >= 0))
    cb = jnp.repeat(rel_bias[REL_BUCKETS - 1].reshape(hk, Q_PER_KV), qb, axis=1).reshape(hk, 1, w)

    per_g = lambda shape: pl.BlockSpec((None,) + shape, lambda g, i: (g,) + (0,) * len(shape))
    per_gi = pl.BlockSpec((None, None, qb, qb), lambda g, i: (g, i, 0, 0))
    return pl.pallas_call(
        _nsa_prompt_kernel, grid=(hk, nb),
        in_specs=[pl.BlockSpec((qb, w), lambda g, i: (i, C_Q // w + g)),
                  per_gi, per_gi, pl.BlockSpec((None, qb, qb), lambda g, i: (i, 0, 0)),
                  per_g((4, qb, qb)), per_g((4, qb, qb)), pl.BlockSpec((4, qb, qb), lambda g, i: (0, 0, 0)),
                  per_g((nb, qb, qb)), per_g((nb, qb, qb)), per_g((nb, qb, qb)), per_g((nb, qb, qb)),
                  pl.BlockSpec((qb, 256), lambda g, i: (i, C_SMALL // 256)),
                  per_g((qb, w)), per_g((qb, w)), per_g((qb, w)), per_g((1, w))],
        out_specs=pl.BlockSpec((qb, w), lambda g, i: (i, g)),
        out_shape=jax.ShapeDtypeStruct((seq, NSA_HEADS * HEAD_DIM), BF16),
        scratch_shapes=[pltpu.VMEM((1, w), F32), pltpu.VMEM((1, w), F32), pltpu.VMEM((HEAD_DIM, w), F32),
                        pltpu.VMEM((qb, w), F32), pltpu.VMEM((HEAD_DIM, w), F32), pltpu.VMEM((qb, qb), F32),
                        pltpu.VMEM((256, qb), F32)],
        compiler_params=_cp("parallel", "arbitrary"), name="nsa_prompt")(
            proj, kcw, vcw, wselw, kco, vco, wselo, ks, vs, kw, vw, proj, tc, td, tp, cb)


N_BLK_S = -(-(PAST_LEN + 1) // SEL_BLOCK)
N_OLD_SEL = N_SEL - 1


def _nsa_sample_cmp_kernel(q_ref, kcv_ref, bias_ref, wsel_ref, oc_ref, idx_ref):
    hrow = lax.broadcasted_iota(jnp.int32, (NSA_HEADS, 128), 0) // Q_PER_KV
    psums = []
    bias = bias_ref[...]
    valid = bias > 0.5 * NEG
    for bb in range(q_ref.shape[0]):
        q = q_ref[bb].astype(BF16)
        o_c = jnp.zeros((NSA_HEADS, HEAD_DIM), F32)
        for g in range(NSA_KV_HEADS):
            s = jnp.where(valid, _dot_nt(q, kcv_ref[bb, 2 * g]) * SCALE + bias, NEG)
            mx = jnp.max(s, axis=-1, keepdims=True)
            e = jnp.where(valid, jnp.exp(s - mx), 0.0)
            p = e / jnp.maximum(jnp.sum(e, axis=-1, keepdims=True), 1e-30)
            mine = hrow == g
            o_c = o_c + jnp.where(mine, _dot(p.astype(BF16), kcv_ref[bb, 2 * g + 1]), 0.0)
            psums.append(jnp.sum(jnp.where(mine, p, 0.0), axis=0, keepdims=True))
        oc_ref[bb] = o_c
    psum = jnp.concatenate(psums, axis=0)
    imp = _dot(psum.astype(BF16), wsel_ref[...])
    lane = lax.broadcasted_iota(jnp.int32, imp.shape, 1)
    cur = N_BLK_S - 1
    forced = (lane == 0) | (lane == cur - 1)
    score = jnp.where(lane >= cur, -3e38, jnp.where(forced, FORCED_SCORE, imp))
    lanef = lane.astype(F32)
    out = jnp.zeros(imp.shape, F32)
    for r in range(N_OLD_SEL):
        mx = jnp.max(score, axis=-1, keepdims=True)
        idx = jnp.min(jnp.where(score == mx, lanef, 1e9), axis=-1, keepdims=True)
        out = jnp.where(lane == r, idx, out)
        score = jnp.where(lanef == idx, -3e38, score)
    idx_ref[...] = out.astype(jnp.int32)


def _nsa_sample_cmp(q_s, kcv_s, rel_bias):
    b = DEC_BATCH
    rbt = rel_bias.T
    n = np.arange(128)[None, :]
    dist = PAST_LEN - (CMP_STRIDE * n + 2 * CMP_STRIDE - 1)
    bias = _bias_table(rbt, dist, n < (PAST_LEN // CMP_STRIDE - 1))[:, 0, :]
    wsel = jnp.asarray(_wsel_np(PAST_LEN // CMP_STRIDE - 1, N_BLK_S, 128, 0), BF16)
    bb = 8
    return pl.pallas_call(
        _nsa_sample_cmp_kernel, grid=(b // bb,),
        in_specs=[pl.BlockSpec((bb, NSA_HEADS, HEAD_DIM), lambda i: (i, 0, 0)),
                  pl.BlockSpec((bb, 2 * NSA_KV_HEADS, 128, HEAD_DIM), lambda i: (i, 0, 0, 0)),
                  pl.BlockSpec((NSA_HEADS, 128), lambda i: (0, 0)),
                  pl.BlockSpec((128, 128), lambda i: (0, 0))],
        out_specs=[pl.BlockSpec((bb, NSA_HEADS, HEAD_DIM), lambda i: (i, 0, 0)),
                   pl.BlockSpec((bb * NSA_KV_HEADS, 128), lambda i: (i, 0))],
        out_shape=[jax.ShapeDtypeStruct((b, NSA_HEADS, HEAD_DIM), F32),
                   jax.ShapeDtypeStruct((b * NSA_KV_HEADS, 128), jnp.int32)],
        compiler_params=_cp("parallel"), name="nsa_sample_cmp")(q_s, kcv_s, bias, wsel)


def _nsa_sample_attn_kernel(pt_ref, idx_ref, *refs):
    del pt_ref
    blocks = refs[:N_OLD_SEL]
    (win_ref, news_ref, neww_ref, q_ref, oc_ref, gn_ref, bs_ref, bw_ref, b0_ref, o_ref) = refs[N_OLD_SEL:]
    b = pl.program_id(0)
    g = pl.program_id(1)
    q = q_ref[...].astype(BF16)
    gates = jax.nn.sigmoid(gn_ref[...])
    b0 = b0_ref[...]

    def attend(pieces, new_ref):
        new = new_ref[...]
        s_new = jnp.sum(q.astype(F32) * new[:, :HEAD_DIM].astype(BF16).astype(F32), axis=-1, keepdims=True)
        s_new = s_new * SCALE + b0
        mx = s_new
        for s, valid, _ in pieces:
            mx = jnp.maximum(mx, jnp.max(s, axis=-1, keepdims=True))
        e_new = jnp.exp(s_new - mx)
        den = e_new
        acc = e_new.astype(BF16).astype(F32) * new[:, HEAD_DIM:].astype(BF16).astype(F32)
        for s, valid, v in pieces:
            e = jnp.exp(s - mx)
            if valid is not None:
                e = jnp.where(valid, e, 0.0)
            den = den + jnp.sum(e, axis=-1, keepdims=True)
            acc = acc + _dot(e.astype(BF16), v)
        return acc / jnp.maximum(den, 1e-30)

    def interleaved(ref, rows):
        kv = ref[...].reshape(2 * rows, HEAD_DIM)
        return kv.astype(BF16), pltpu.roll(kv, 2 * rows - 1, axis=0).astype(BF16)

    pieces = []
    for kk in range(N_OLD_SEL):
        kvb, vsb = interleaved(blocks[kk], SEL_BLOCK)
        m = idx_ref[(b * NSA_KV_HEADS + g) * N_OLD_SEL + kk]
        s = _dot_nt(q, kvb) * SCALE + bs_ref[m]
        pieces.append((s, None, vsb))
    o_s = attend(pieces, news_ref)

    kvb, vsb = interleaved(win_ref, WINDOW)
    bw = bw_ref[...]
    valid_w = bw > 0.5 * NEG
    s_w = jnp.where(valid_w, _dot_nt(q, kvb) * SCALE + bw, NEG)
    o_w = attend([(s_w, valid_w, vsb)], neww_ref)

    out = gates[:, 0:1] * oc_ref[...] + gates[:, 1:2] * o_s + gates[:, 2:3] * o_w
    o_ref[...] = out.astype(o_ref.dtype)


def _nsa_sample_attn(layer, cache_sel, cache_win, page_table, idx, kvs_new, kvw_new, q_s, o_c, gn_s, rel_bias):
    b = DEC_BATCH
    hk = NSA_KV_HEADS
    rbt = rel_bias.T
    m = np.arange(N_BLK_S - 1)[:, None]
    r = np.arange(SEL_BLOCK)[None, :]
    bs = _bias_table(rbt, PAST_LEN - (SEL_BLOCK * m + r), np.ones((N_BLK_S - 1, SEL_BLOCK), bool))
    bs = bs.reshape(hk, Q_PER_KV, N_BLK_S - 1, SEL_BLOCK).transpose(0, 2, 1, 3)
    odd_masked = lambda t: jnp.stack([t, jnp.full(t.shape, NEG, F32)], axis=-1).reshape(t.shape[:-1] + (-1,))
    bs = odd_masked(bs)
    rw = np.arange(WINDOW)[None, :]
    bw = odd_masked(_bias_table(rbt, WINDOW - rw, rw >= 1)[:, 0, :].reshape(hk, Q_PER_KV, WINDOW))
    b0 = rel_bias[0].reshape(hk, Q_PER_KV, 1)
    per_page = PAGE_SIZE // SEL_BLOCK
    csel = cache_sel.reshape(DEPTH, -1, per_page, SEL_BLOCK, hk, 2, HEAD_DIM)

    def sel_spec(kk):
        def imap(bi, g, pt, ix):
            blk = ix[(bi * hk + g) * N_OLD_SEL + kk]
            return (layer, pt[bi * N_PAGES + blk // per_page], blk % per_page, 0, g, 0, 0)
        return pl.BlockSpec((None, None, None, SEL_BLOCK, None, 2, HEAD_DIM), imap)

    bgj = lambda last: pl.BlockSpec((None, None, Q_PER_KV, last), lambda bi, g, pt, ix: (bi, g, 0, 0))
    new_spec = pl.BlockSpec((None, None, 1, 2 * HEAD_DIM), lambda bi, g, pt, ix: (bi, g, 0, 0))
    in_specs = [sel_spec(kk) for kk in range(N_OLD_SEL)] + [
        pl.BlockSpec((None, None, WINDOW, None, 2, HEAD_DIM), lambda bi, g, pt, ix: (layer, bi, 0, g, 0, 0)),
        new_spec, new_spec, bgj(HEAD_DIM), bgj(HEAD_DIM), bgj(3),
        pl.BlockSpec((None, N_BLK_S - 1, Q_PER_KV, 2 * SEL_BLOCK), lambda bi, g, pt, ix: (g, 0, 0, 0)),
        pl.BlockSpec((None, Q_PER_KV, 2 * WINDOW), lambda bi, g, pt, ix: (g, 0, 0)),
        pl.BlockSpec((None, Q_PER_KV, 1), lambda bi, g, pt, ix: (g, 0, 0))]
    gs = pltpu.PrefetchScalarGridSpec(
        num_scalar_prefetch=2, grid=(b, hk), in_specs=in_specs,
        out_specs=bgj(HEAD_DIM))
    return pl.pallas_call(
        _nsa_sample_attn_kernel, grid_spec=gs,
        out_shape=jax.ShapeDtypeStruct((b, hk, Q_PER_KV, HEAD_DIM), F32),
        compiler_params=_cp("parallel", "parallel"), name="nsa_sample_attn")(
            page_table.reshape(-1), idx.reshape(-1), *([csel] * N_OLD_SEL), cache_win,
            kvs_new.reshape(b, hk, 1, 2 * HEAD_DIM), kvw_new.reshape(b, hk, 1, 2 * HEAD_DIM),
            q_s.reshape(b, hk, Q_PER_KV, HEAD_DIM), o_c.reshape(b, hk, Q_PER_KV, HEAD_DIM), gn_s, bs, bw, b0)


def _router_kernel(h_ref, w_ref, b_ref, o_ref):
    logits = _dot(h_ref[...], w_ref[...]) + b_ref[...]
    lane = lax.broadcasted_iota(jnp.int32, logits.shape, 1)
    lanef = lane.astype(F32)

    def softmax(mask):
        s = jnp.where(mask, logits, NEG)
        e = jnp.where(mask, jnp.exp(s - jnp.max(s, axis=-1, keepdims=True)), 0.0)
        return e / jnp.sum(e, axis=-1, keepdims=True)

    def top1(p, mask):
        mx = jnp.max(jnp.where(mask, p, -1.0), axis=-1, keepdims=True)
        idx = jnp.min(jnp.where(mask & (p == mx), lanef, 1e9), axis=-1, keepdims=True)
        return mx, idx

    gmask = (lane >= N_EXPERTS) & (lane < N_EXPERTS + MOE_GROUPS)
    pg = softmax(gmask)
    p_top, g_idx = top1(pg, gmask)
    e_lo = (g_idx - N_EXPERTS) * MOE_PER_GROUP
    emask = (lanef >= e_lo) & (lanef < e_lo + MOE_PER_GROUP)
    pe = softmax(emask)
    w_a, i_a = top1(pe, emask)
    mask_b = emask & (lanef != i_a)
    w_b, i_b = top1(pe, mask_b)
    tot = w_a + w_b
    gate = jnp.where(lanef == i_a, w_a / tot * p_top, jnp.where(lanef == i_b, w_b / tot * p_top, 0.0))
    o_ref[...] = gate


def _router(hn, w_group, b_group, w_exp, b_exp):
    w = jnp.concatenate([w_exp, w_group, jnp.zeros((D_MODEL, 128 - N_EXPERTS - MOE_GROUPS), F32)], axis=1)
    bias = jnp.concatenate([b_exp, b_group, jnp.zeros((128 - N_EXPERTS - MOE_GROUPS,), F32)]).reshape(1, 128)
    return pl.pallas_call(
        _router_kernel, grid=(M_ALL // TM,),
        in_specs=[pl.BlockSpec((TM, D_MODEL), lambda i: (i, 0)),
                  pl.BlockSpec((D_MODEL, 128), lambda i: (0, 0)),
                  pl.BlockSpec((1, 128), lambda i: (0, 0))],
        out_specs=pl.BlockSpec((TM, 128), lambda i: (i, 0)),
        out_shape=jax.ShapeDtypeStruct((M_ALL, 128), F32),
        compiler_params=_cp("parallel"), name="moe_router")(hn, w.astype(BF16), bias)


def _moe_kernel(h_ref, win_ref, wout_ref, gate_ref, x_ref, o_ref, acc_scr):
    e = pl.program_id(1)

    @pl.when(e == 0)
    def _():
        acc_scr[...] = jnp.zeros(acc_scr.shape, F32)

    u = _dot(h_ref[...], win_ref[...])
    gate = gate_ref[...]
    lane = lax.broadcasted_iota(jnp.int32, gate.shape, 1)
    ge = jnp.sum(jnp.where(lane == e, gate, 0.0), axis=-1, keepdims=True)
    act = _silu(u[:, :D_EXPERT]) * u[:, D_EXPERT:] * ge
    acc_scr[...] = acc_scr[...] + _dot(act.astype(BF16), wout_ref[...])

    @pl.when(e == N_EXPERTS - 1)
    def _():
        o_ref[...] = x_ref[...] + acc_scr[...]


def _moe(hn, gate, x, w_in_e, w_out_e):
    return pl.pallas_call(
        _moe_kernel, grid=(M_ALL // TM, N_EXPERTS),
        in_specs=[pl.BlockSpec((TM, D_MODEL), lambda i, e: (i, 0)),
                  pl.BlockSpec((None, D_MODEL, 2 * D_EXPERT), lambda i, e: (e, 0, 0)),
                  pl.BlockSpec((None, D_EXPERT, D_MODEL), lambda i, e: (e, 0, 0)),
                  pl.BlockSpec((TM, 128), lambda i, e: (i, 0)),
                  pl.BlockSpec((TM, D_MODEL), lambda i, e: (i, 0))],
        out_specs=pl.BlockSpec((TM, D_MODEL), lambda i, e: (i, 0)),
        out_shape=jax.ShapeDtypeStruct((M_ALL, D_MODEL), F32),
        scratch_shapes=[pltpu.VMEM((TM, D_MODEL), F32)],
        compiler_params=_cp("parallel", "arbitrary"), name="moe_experts")(hn, w_in_e, w_out_e, gate, x)


def _prep_w_in(w):
    offs = np.cumsum((0,) + IN_SIZES)
    z, xbc, dt, q, kvc, kvs, kvw, gn, gm = [w[:, offs[k]:offs[k + 1]] for k in range(len(IN_SIZES))]
    small = jnp.concatenate([dt, gn, jnp.zeros((D_MODEL, 256 - dt.shape[1] - gn.shape[1]), w.dtype)], axis=1)
    return jnp.concatenate([z, xbc, q, kvc, kvs, kvw, gm, small], axis=1).astype(BF16)


def _layer(l, x, p, cache_cmp_kv, cache_sel_kv, cache_win_kv, state_ssm_l, state_conv_l, page_table, rel_bias):
    s = SEQ
    b = DEC_BATCH
    h1 = _rmsnorm(x, p['norm1_w'], BF16)
    proj = _mm(h1, _prep_w_in(p['w_in']), tn=1792, out_dtype=F32, name="in_proj")

    kv_new = {name: proj[:, c:c + KV_COLS] for name, c in (('cmp', C_KVC), ('sel', C_KVS), ('win', C_KVW))}

    act_p = _conv_prompt(proj, p['conv_w'], p['conv_b'])
    bt = act_p[:, D_SSD:D_SSD + SSD_GROUPS * SSD_STATE].T.astype(BF16)
    dtr = proj[:, C_SMALL:C_SMALL + SSD_HEADS]
    y_p, ssm_p = _ssd_scan_prompt(act_p, bt, dtr[:s], p['dt_bias'], p['a_log'], p['d_skip'])
    conv_p = proj[s - (SSD_CONV - 1):s, C_XBC:C_XBC + CONV_CH]

    xbc_s = proj[s:, C_XBC:C_XBC + CONV_CH]
    act_s = _conv_sample(state_conv_l.transpose(1, 0, 2), xbc_s, p['conv_w'], p['conv_b'])
    state_t = state_ssm_l.reshape(b, D_SSD, SSD_STATE).transpose(0, 2, 1)
    y_s, h_s = _ssd_step_sample(state_t, act_s, dtr[s:], p['dt_bias'], p['a_log'], p['d_skip'])
    ssm_s = h_s.transpose(0, 2, 1).reshape(b, SSD_HEADS, SSD_HEADDIM, SSD_STATE)
    conv_s = jnp.concatenate([state_conv_l[:, 1:], xbc_s[:, None, :]], axis=1)

    yn = _gated_norm(y_p, y_s.reshape(b, D_SSD), proj, p['ssd_norm_w'])
    y_a = _mm(yn, p['w_ssd_out'].astype(BF16), tn=1024, out_dtype=F32, name="ssd_out")

    b1c = _cmp_bias(p['cmp_pe'], p['cmp_w1'], p['cmp_b1'])
    w1c = (p['cmp_w1'].reshape(2, 2, CMP_STRIDE * HEAD_DIM, HEAD_DIM).transpose(0, 2, 1, 3)
           .reshape(2, CMP_STRIDE * HEAD_DIM, 2 * HEAD_DIM).astype(BF16))
    w2 = p['cmp_w2'].astype(BF16)

    n_grp_p = s // (N_PAGES * PAGE_SIZE)
    kvc_pages = kv_new['cmp'][:s].reshape(1, s // PAGE_SIZE, PAGE_SIZE, NSA_KV_HEADS, 2, HEAD_DIM)
    pt_p = jnp.arange(n_grp_p * N_PAGES, dtype=jnp.int32)
    pp = _chunk_proj(kvc_pages, 0, pt_p, w1c, n_grp_p)
    kcv_p = _cmp_mlp(pp, b1c, w2, 1, F32)[0]
    attn_p = _nsa_prompt(proj, kcv_p, kv_new['sel'][:s], kv_new['win'][:s], rel_bias, s)

    ps = _chunk_proj(cache_cmp_kv, l, page_table.reshape(-1), w1c, b)
    kcv_s = _cmp_mlp(ps, b1c, w2, b, BF16)
    q_s = proj[s:, C_Q:C_Q + NSA_HEADS * HEAD_DIM].reshape(b, NSA_HEADS, HEAD_DIM)
    o_c, idx = _nsa_sample_cmp(q_s, kcv_s, rel_bias)
    gn_s = (proj[s:, C_SMALL + GATE_LANE0:C_SMALL + GATE_LANE0 + 3 * NSA_HEADS]
            .reshape(b, 3, NSA_KV_HEADS, Q_PER_KV).transpose(0, 2, 3, 1))
    attn_s = _nsa_sample_attn(l, cache_sel_kv, cache_win_kv, page_table, idx[:, :N_OLD_SEL],
                              kv_new['sel'][s:], kv_new['win'][s:], q_s, o_c, gn_s, rel_bias)
    attn = jnp.concatenate([attn_p, attn_s.reshape(b, NSA_HEADS * HEAD_DIM).astype(BF16)], axis=0)

    tn = 1024
    m = _mm(attn, p['w_nsa_out'].astype(BF16), tn=tn, out_dtype=BF16, name="nsa_out_merge",
            extras=((y_a, 0), (proj, C_GM // tn), (proj, (C_GM + D_MODEL) // tn)), epilogue=_merge_epilogue)
    x = _mm(m, p['w_out'].astype(BF16), tn=tn, out_dtype=F32, name="out_proj", extras=((x, 0),),
            epilogue=_resid_epilogue)
    h2 = _rmsnorm(x, p['norm2_w'], BF16)
    gate = _router(h2, p['w_router_group'], p['b_router_group'], p['w_router_exp'], p['b_router_exp'])
    x = _moe(h2, gate, x, p['w_exp_in'].astype(BF16), p['w_exp_out'].astype(BF16))

    kv5 = lambda a: a.reshape(a.shape[0], NSA_KV_HEADS, 2, HEAD_DIM)
    win_s = jnp.concatenate([cache_win_kv[l][:, 1:], kv5(kv_new['win'][s:])[:, None]], axis=1)
    outs = dict(
        cmp_p=kv5(kv_new['cmp'][:s])[None], cmp_s=kv5(kv_new['cmp'][s:])[:, None],
        sel_p=kv5(kv_new['sel'][:s])[None], sel_s=kv5(kv_new['sel'][s:])[:, None],
        win_p=kv5(kv_new['win'][s - WINDOW:s])[None], win_s=win_s,
        ssm_p=ssm_p[None], ssm_s=ssm_s, conv_p=conv_p[None], conv_s=conv_s)
    return x, outs


def kernel(x_prompt, x_sample, cache_cmp_kv, cache_sel_kv, cache_win_kv, state_ssm, state_conv, page_table,
           rel_bias, norm1_w, norm2_w, final_norm_w, w_in, conv_w, conv_b, dt_bias, a_log, d_skip, ssd_norm_w,
           w_ssd_out, cmp_pe, cmp_w1, cmp_b1, cmp_w2, w_nsa_out, w_out, w_router_group, b_router_group,
           w_router_exp, b_router_exp, w_exp_in, w_exp_out):
    x = jnp.concatenate([x_prompt[0], x_sample[:, 0]], axis=0)
    per_layer = []
    for l in range(DEPTH):
        p = {'norm1_w': norm1_w[l], 'norm2_w': norm2_w[l], 'w_in': w_in[l], 'conv_w': conv_w[l],
             'conv_b': conv_b[l], 'dt_bias': dt_bias[l], 'a_log': a_log[l], 'd_skip': d_skip[l],
             'ssd_norm_w': ssd_norm_w[l], 'w_ssd_out': w_ssd_out[l], 'cmp_pe': cmp_pe[l], 'cmp_w1': cmp_w1[l],
             'cmp_b1': cmp_b1[l], 'cmp_w2': cmp_w2[l], 'w_nsa_out': w_nsa_out[l], 'w_out': w_out[l],
             'w_router_group': w_router_group[l], 'b_router_group': b_router_group[l],
             'w_router_exp': w_router_exp[l], 'b_router_exp': b_router_exp[l],
             'w_exp_in': w_exp_in[l], 'w_exp_out': w_exp_out[l]}
        x, outs = _layer(l, x, p, cache_cmp_kv, cache_sel_kv, cache_win_kv, state_ssm[l], state_conv[l],
                         page_table, rel_bias)
        per_layer.append(outs)
    y = _rmsnorm(x, final_norm_w, F32)
    stack = lambda k: jnp.stack([o[k] for o in per_layer])
    return (y[:SEQ][None], y[SEQ:][:, None], stack('cmp_p'), stack('cmp_s'), stack('sel_p'), stack('sel_s'),
            stack('win_p'), stack('win_s'), stack('ssm_p'), stack('ssm_s'), stack('conv_p'), stack('conv_s'))
```

```python
import functools
import math

import numpy as np
import jax
import jax.numpy as jnp
from jax import lax
from jax.experimental import pallas as pl
from jax.experimental.pallas import tpu as pltpu

F32 = jnp.float32
BF16 = jnp.bfloat16
NEG = -1e30

D_MODEL = 2048
SEQ = 8192
DEPTH = 2
DEC_BATCH = 128
PAST_LEN = 2048
PAGE_SIZE = 128
N_PAGES = PAST_LEN // PAGE_SIZE
D_SSD = 4096
SSD_HEADDIM = 64
SSD_HEADS = 64
SSD_GROUPS = 8
SSD_STATE = 128
SSD_CONV = 4
SSD_CHUNK = 256
CONV_CH = 6144
NSA_HEADS = 16
NSA_KV_HEADS = 4
HEAD_DIM = 128
Q_PER_KV = 4
CMP_STRIDE = 16
SEL_BLOCK = 64
N_SEL = 16
WINDOW = 512
Q_BLOCK = 128
FORCED_SCORE = 1e4
REL_BUCKETS = 32
REL_MAX_DIST = 128
MOE_GROUPS = 4
MOE_PER_GROUP = 4
N_EXPERTS = 16
D_EXPERT = 512
RMS_EPS = 1e-6
KV_COLS = 1024
IN_SIZES = (D_SSD, CONV_CH, SSD_HEADS, NSA_HEADS * HEAD_DIM, KV_COLS, KV_COLS, KV_COLS, 3 * NSA_HEADS, 2 * D_MODEL)
SCALE = HEAD_DIM ** -0.5
LOG2E = math.log2(math.e)

C_Z = 0
C_XBC = 4096
C_Q = 10240
C_KVC = 12288
C_KVS = 13312
C_KVW = 14336
C_GM = 15360
C_SMALL = 19456
N_PROJ = 19712
GATE_LANE0 = 64

M_ALL = SEQ + DEC_BATCH
TM = 640
VMEM_LIMIT_MIB = 56


def _cp(*sem):
    return pltpu.CompilerParams(dimension_semantics=sem, vmem_limit_bytes=VMEM_LIMIT_MIB * 1024 * 1024)


def _dot(a, b):
    return jnp.dot(a, b, preferred_element_type=F32)


def _dot_nt(a, b):
    return lax.dot_general(a, b, (((1,), (1,)), ((), ())), preferred_element_type=F32)


def _dot_f32(a, b):
    return jnp.dot(a, b, preferred_element_type=F32, precision=lax.Precision.HIGHEST)


def _silu(x):
    return x * jax.nn.sigmoid(x)


def _softplus(x):
    return jnp.maximum(x, 0.0) + jnp.log1p(jnp.exp(-jnp.abs(x)))


def _split_hi_lo(v):
    hi = v.astype(BF16)
    lo = (v - hi.astype(F32)).astype(BF16)
    return hi, lo


def _expand(v, e):
    hi, lo = _split_hi_lo(v)
    return _dot(hi, e) + _dot(lo, e)


def _rmsnorm_kernel(x_ref, w_ref, o_ref):
    x = x_ref[...]
    ms = jnp.mean(x * x, axis=-1, keepdims=True)
    o_ref[...] = (x * lax.rsqrt(ms + RMS_EPS) * w_ref[...]).astype(o_ref.dtype)


def _rmsnorm(x, w, out_dtype, tm=TM):
    m, d = x.shape
    return pl.pallas_call(
        _rmsnorm_kernel, grid=(m // tm,),
        in_specs=[pl.BlockSpec((tm, d), lambda i: (i, 0)), pl.BlockSpec((1, d), lambda i: (0, 0))],
        out_specs=pl.BlockSpec((tm, d), lambda i: (i, 0)),
        out_shape=jax.ShapeDtypeStruct((m, d), out_dtype),
        compiler_params=_cp("parallel"), name="rmsnorm")(x, w.reshape(1, d))


def _gated_norm_kernel(yp_ref, ys_ref, z_ref, w_ref, o_ref, *, n_prompt_tiles):
    i = pl.program_id(0)
    y = jnp.where(i < n_prompt_tiles, yp_ref[...], ys_ref[...])
    gated = y * _silu(z_ref[...])
    ms = jnp.mean(gated * gated, axis=-1, keepdims=True)
    o_ref[...] = (gated * lax.rsqrt(ms + RMS_EPS) * w_ref[...]).astype(o_ref.dtype)


def _gated_norm(y_p, y_s, proj, w):
    tm = DEC_BATCH
    npt = SEQ // tm
    return pl.pallas_call(
        functools.partial(_gated_norm_kernel, n_prompt_tiles=npt), grid=(M_ALL // tm,),
        in_specs=[pl.BlockSpec((tm, D_SSD), lambda i: (jnp.minimum(i, npt - 1), 0)),
                  pl.BlockSpec((tm, D_SSD), lambda i: (0, 0)),
                  pl.BlockSpec((tm, D_SSD), lambda i: (i, C_Z // D_SSD)),
                  pl.BlockSpec((1, D_SSD), lambda i: (0, 0))],
        out_specs=pl.BlockSpec((tm, D_SSD), lambda i: (i, 0)),
        out_shape=jax.ShapeDtypeStruct((M_ALL, D_SSD), BF16),
        compiler_params=_cp("parallel"), name="gated_norm")(y_p, y_s, proj, w.reshape(1, D_SSD))


def _mm_kernel(a_ref, w_ref, *rest, epilogue):
    o_ref = rest[-1]
    acc = _dot(a_ref[...], w_ref[...])
    if epilogue is not None:
        acc = epilogue(acc, *[r[...] for r in rest[:-1]])
    o_ref[...] = acc.astype(o_ref.dtype)


def _mm(a, w, *, tn, out_dtype, name, extras=(), epilogue=None, tm=TM):
    m, k = a.shape
    n = w.shape[1]
    in_specs = [pl.BlockSpec((tm, k), lambda j, i: (i, 0)), pl.BlockSpec((k, tn), lambda j, i: (0, j))]
    args = [a, w]
    for arr, coff in extras:
        in_specs.append(pl.BlockSpec((tm, tn), lambda j, i, coff=coff: (i, coff + j)))
        args.append(arr)
    return pl.pallas_call(
        functools.partial(_mm_kernel, epilogue=epilogue), grid=(n // tn, m // tm),
        in_specs=in_specs, out_specs=pl.BlockSpec((tm, tn), lambda j, i: (i, j)),
        out_shape=jax.ShapeDtypeStruct((m, n), out_dtype),
        compiler_params=_cp("parallel", "parallel"), name=name)(*args)


def _merge_epilogue(y_b, y_a, g_a, g_b):
    return jax.nn.sigmoid(g_a) * y_a + jax.nn.sigmoid(g_b) * y_b


def _resid_epilogue(acc, x):
    return x + acc


CONV_TR = 512
CONV_TC = 512


def _conv_prompt_kernel(x_ref, halo_ref, w_ref, b_ref, o_ref):
    i = pl.program_id(1)
    halo = jnp.where(i > 0, halo_ref[...], 0.0)
    x = jnp.concatenate([halo, x_ref[...]], axis=0)
    n = x.shape[0]
    acc = b_ref[...] + x[8:] * w_ref[SSD_CONV - 1:SSD_CONV, :]
    for k in range(SSD_CONV - 1):
        shifted = pltpu.roll(x, SSD_CONV - 1 - k, axis=0)
        acc = acc + shifted[8:] * w_ref[k:k + 1, :]
    del n
    o_ref[...] = _silu(acc)


def _conv_prompt(proj, conv_w, conv_b):
    nct = CONV_CH // CONV_TC
    nrt = SEQ // CONV_TR
    c0 = C_XBC // CONV_TC
    return pl.pallas_call(
        _conv_prompt_kernel, grid=(nct, nrt),
        in_specs=[pl.BlockSpec((CONV_TR, CONV_TC), lambda c, i: (i, c0 + c)),
                  pl.BlockSpec((8, CONV_TC), lambda c, i: (jnp.maximum(i * (CONV_TR // 8) - 1, 0), c0 + c)),
                  pl.BlockSpec((SSD_CONV, CONV_TC), lambda c, i: (0, c)),
                  pl.BlockSpec((1, CONV_TC), lambda c, i: (0, c))],
        out_specs=pl.BlockSpec((CONV_TR, CONV_TC), lambda c, i: (i, c)),
        out_shape=jax.ShapeDtypeStruct((SEQ, CONV_CH), F32),
        compiler_params=_cp("parallel", "parallel"), name="conv_prompt")(
            proj, proj, conv_w, conv_b.reshape(1, CONV_CH))


GW = SSD_HEADS // SSD_GROUPS * SSD_HEADDIM
HPG = SSD_HEADS // SSD_GROUPS


def _scan_kernel(xs_ref, b_ref, c_ref, bt_ref, dtr_ref, dtrt_ref, dtb_r_ref, dtb_c_ref, al_r_ref, al_c_ref,
                 dskip_ref, e_ref, y_ref, hout_ref, h_scr):
    c = pl.program_id(1)
    q = SSD_CHUNK

    @pl.when(c == 0)
    def _():
        h_scr[...] = jnp.zeros((SSD_STATE, GW), F32)

    dt = _softplus(dtr_ref[...] + dtb_r_ref[...])
    d_a = dt * (-jnp.exp(al_r_ref[...]))
    dt_t = _softplus(dtrt_ref[...] + dtb_c_ref[...])
    d_a_t = dt_t * (-jnp.exp(al_c_ref[...]))
    row = lax.broadcasted_iota(jnp.int32, (q, q), 0)
    col = lax.broadcasted_iota(jnp.int32, (q, q), 1)
    tril = row >= col
    cum = _dot_f32(tril.astype(F32), d_a)
    cum_t = _dot_f32(d_a_t, (col >= row).astype(F32))
    cum_last = cum[q - 1:q, :]
    e = e_ref[...]
    dt_e = _expand(dt, e)
    ecum_e = _expand(jnp.exp(cum), e)
    tail_e = _expand(jnp.exp(cum_last - cum), e)
    elast_e = _expand(jnp.broadcast_to(jnp.exp(cum_last), (8, 128)), e)[0:1, :]

    xs = xs_ref[...]
    xdt = xs * dt_e
    xdt_b = xdt.astype(BF16)
    xw = (xdt * tail_e).astype(BF16)
    bg = b_ref[...].astype(BF16)
    cg = c_ref[...].astype(BF16)
    cb = _dot_nt(cg, bg)
    h_prev = h_scr[...]
    y = _dot(cg, h_prev.astype(BF16)) * ecum_e
    lane = lax.broadcasted_iota(jnp.int32, (q, 128), 1)
    ys = []
    for pair in range(HPG // 2):
        xpair = xdt_b[:, pair * 128:(pair + 1) * 128]
        top = jnp.where(lane < SSD_HEADDIM, xpair, jnp.zeros_like(xpair))
        bot = jnp.where(lane >= SSD_HEADDIM, xpair, jnp.zeros_like(xpair))
        atts = []
        for hh in range(2):
            h8 = pair * 2 + hh
            seg = jnp.exp(jnp.where(tril, cum[:, h8:h8 + 1] - cum_t[h8:h8 + 1, :], NEG))
            atts.append((cb * seg).astype(BF16))
        ys.append(_dot(jnp.concatenate(atts, axis=1), jnp.concatenate([top, bot], axis=0)))
    y = y + jnp.concatenate(ys, axis=1) + dskip_ref[...] * xs
    y_ref[...] = y
    h_new = h_prev * elast_e + _dot(bt_ref[...], xw)
    h_scr[...] = h_new

    @pl.when(c == pl.num_programs(1) - 1)
    def _():
        hout_ref[...] = h_new


def _head_expand_matrix(n_heads, rows):
    e = np.zeros((rows, n_heads * SSD_HEADDIM), np.float32)
    for h in range(n_heads):
        e[h, h * SSD_HEADDIM:(h + 1) * SSD_HEADDIM] = 1.0
    return jnp.asarray(e, BF16)


def _ssd_scan_prompt(act, bt, dtr, dt_bias, a_log, d_skip):
    q = SSD_CHUNK
    nc = SEQ // q
    dtr_g = dtr.reshape(SEQ, SSD_GROUPS, HPG).transpose(1, 0, 2)
    dtr_g128 = jnp.pad(dtr_g, ((0, 0), (0, 0), (0, 128 - HPG)))
    dtrt_g = dtr_g.transpose(0, 2, 1)
    pad_r = lambda v: jnp.pad(v.reshape(SSD_GROUPS, 1, HPG), ((0, 0), (0, 0), (0, 128 - HPG)))
    col_c = lambda v: v.reshape(SSD_GROUPS, HPG, 1)
    dskip_e = jnp.repeat(d_skip, SSD_HEADDIM).reshape(1, D_SSD)
    e = _head_expand_matrix(HPG, 128)
    small = lambda shape: pl.BlockSpec((None,) + shape, lambda g, c: (g, 0, 0))
    y, h_out = pl.pallas_call(
        _scan_kernel, grid=(SSD_GROUPS, nc),
        in_specs=[pl.BlockSpec((q, GW), lambda g, c: (c, g)),
                  pl.BlockSpec((q, SSD_STATE), lambda g, c: (c, D_SSD // SSD_STATE + g)),
                  pl.BlockSpec((q, SSD_STATE), lambda g, c: (c, (D_SSD + SSD_GROUPS * SSD_STATE) // SSD_STATE + g)),
                  pl.BlockSpec((SSD_STATE, q), lambda g, c: (g, c)),
                  pl.BlockSpec((None, q, 128), lambda g, c: (g, c, 0)),
                  pl.BlockSpec((None, HPG, q), lambda g, c: (g, 0, c)),
                  small((1, 128)), small((HPG, 1)), small((1, 128)), small((HPG, 1)),
                  pl.BlockSpec((1, GW), lambda g, c: (0, g)),
                  pl.BlockSpec((128, GW), lambda g, c: (0, 0))],
        out_specs=[pl.BlockSpec((q, GW), lambda g, c: (c, g)),
                   pl.BlockSpec((None, SSD_STATE, GW), lambda g, c: (g, 0, 0))],
        out_shape=[jax.ShapeDtypeStruct((SEQ, D_SSD), F32),
                   jax.ShapeDtypeStruct((SSD_GROUPS, SSD_STATE, GW), F32)],
        scratch_shapes=[pltpu.VMEM((SSD_STATE, GW), F32)],
        compiler_params=_cp("parallel", "arbitrary"), name="ssd_scan")(
            act, act, act, bt, dtr_g128, dtrt_g, pad_r(dt_bias), col_c(dt_bias), pad_r(a_log), col_c(a_log),
            dskip_e, e)
    h_last = h_out.reshape(SSD_GROUPS, SSD_STATE, HPG, SSD_HEADDIM).transpose(0, 2, 3, 1)
    return y, h_last.reshape(SSD_HEADS, SSD_HEADDIM, SSD_STATE)


def _conv_sample_kernel(s_ref, x_ref, w_ref, b_ref, o_ref):
    acc = b_ref[...] + x_ref[...] * w_ref[SSD_CONV - 1:SSD_CONV, :]
    for k in range(SSD_CONV - 1):
        acc = acc + s_ref[k] * w_ref[k:k + 1, :]
    o_ref[...] = _silu(acc)


def _conv_sample(state_t, xbc, conv_w, conv_b):
    tc = 1024
    return pl.pallas_call(
        _conv_sample_kernel, grid=(CONV_CH // tc,),
        in_specs=[pl.BlockSpec((SSD_CONV - 1, DEC_BATCH, tc), lambda c: (0, 0, c)),
                  pl.BlockSpec((DEC_BATCH, tc), lambda c: (0, c)),
                  pl.BlockSpec((SSD_CONV, tc), lambda c: (0, c)),
                  pl.BlockSpec((1, tc), lambda c: (0, c))],
        out_specs=pl.BlockSpec((DEC_BATCH, tc), lambda c: (0, c)),
        out_shape=jax.ShapeDtypeStruct((DEC_BATCH, CONV_CH), F32),
        compiler_params=_cp("parallel"), name="conv_sample")(state_t, xbc, conv_w, conv_b.reshape(1, CONV_CH))


def _ssd_step_kernel(h_ref, xs_ref, dtr_ref, bn_ref, cn_ref, dtb_ref, al_ref, dskip_ref, e64_ref, e8_ref,
                     *rest):
    y_ref, hout_ref = rest[-2:]
    n_prev = hout_ref.shape[0] - 1
    if n_prev:
        hout_ref[0:n_prev] = rest[0][...]
    dt = _softplus(dtr_ref[...] + dtb_ref[...])
    dec = jnp.exp(dt * (-jnp.exp(al_ref[...])))
    e64 = e64_ref[...]
    dt_e = _expand(dt, e64)[0:1, :]
    dec_e = _expand(dec, e64)[0:1, :]
    xs = xs_ref[...]
    xdt = xs * dt_e
    e8 = e8_ref[...]
    b_exp = _dot(bn_ref[...].astype(BF16), e8)
    c_exp = _dot(cn_ref[...].astype(BF16), e8)
    h_new = h_ref[...] * dec_e + b_exp * xdt
    hout_ref[n_prev] = h_new
    y_ref[...] = jnp.sum(h_new * c_exp, axis=0, keepdims=True) + dskip_ref[...] * xs


def _ssd_step_sample(state_t, layer, h_acc, act_s, dtr_s, dt_bias, a_log, d_skip):
    b = DEC_BATCH
    xs3 = act_s[:, :D_SSD].reshape(b, 1, D_SSD)
    bn = act_s[:, D_SSD:D_SSD + 1024].reshape(b, SSD_GROUPS, SSD_STATE).transpose(0, 2, 1)
    cn = act_s[:, D_SSD + 1024:].reshape(b, SSD_GROUPS, SSD_STATE).transpose(0, 2, 1)
    bn = jnp.pad(bn, ((0, 0), (0, 0), (0, 128 - SSD_GROUPS)))
    cn = jnp.pad(cn, ((0, 0), (0, 0), (0, 128 - SSD_GROUPS)))
    dtr8 = jnp.pad(dtr_s.reshape(b, 1, SSD_HEADS), ((0, 0), (0, 7), (0, 128 - SSD_HEADS)))
    row128 = lambda v: jnp.pad(v.reshape(1, SSD_HEADS), ((0, 0), (0, 128 - SSD_HEADS)))
    dskip_e = jnp.repeat(d_skip, SSD_HEADDIM).reshape(1, D_SSD)
    e64 = _head_expand_matrix(SSD_HEADS, 128)
    e8np = np.zeros((128, D_SSD), np.float32)
    for g in range(SSD_GROUPS):
        e8np[g, g * GW:(g + 1) * GW] = 1.0
    e8 = jnp.asarray(e8np, BF16)
    full = lambda shape: pl.BlockSpec(shape, lambda i: (0,) * len(shape))
    per_b = lambda shape: pl.BlockSpec((None,) + shape, lambda i: (i, 0, 0))
    per_lb = pl.BlockSpec((None, None, SSD_STATE, D_SSD), lambda i: (layer, i, 0, 0))
    in_specs = [per_lb, per_b((1, D_SSD)), per_b((8, 128)), per_b((SSD_STATE, 128)), per_b((SSD_STATE, 128)),
                full((1, 128)), full((1, 128)), full((1, D_SSD)), full((128, D_SSD)), full((128, D_SSD))]
    args = [state_t, xs3, dtr8, bn, cn, row128(dt_bias), row128(a_log), dskip_e, e64, e8]
    stacked = lambda n: pl.BlockSpec((n, None, SSD_STATE, D_SSD), lambda i: (0, i, 0, 0))
    if h_acc is not None:
        in_specs.append(stacked(layer))
        args.append(h_acc)
    return pl.pallas_call(
        _ssd_step_kernel, grid=(b,), in_specs=in_specs,
        out_specs=[per_b((1, D_SSD)), stacked(layer + 1)],
        out_shape=[jax.ShapeDtypeStruct((b, 1, D_SSD), F32),
                   jax.ShapeDtypeStruct((layer + 1, b, SSD_STATE, D_SSD), F32)],
        compiler_params=_cp("parallel"), name="ssd_step")(*args)


def _chunk_proj_kernel(pt_ref, *refs):
    del pt_ref
    pages = refs[:N_PAGES]
    w_ref, o_ref = refs[N_PAGES:]
    nch = N_PAGES * PAGE_SIZE // CMP_STRIDE
    cpp = PAGE_SIZE // CMP_STRIDE
    for s in range(2):
        lhs_h = []
        for h in range(NSA_KV_HEADS):
            per_r = []
            for r in range(CMP_STRIDE):
                rows = [pg[pl.ds(r, cpp, stride=CMP_STRIDE), h, s, :] for pg in pages]
                per_r.append(jnp.concatenate(rows, axis=0).astype(BF16))
            lhs_h.append(jnp.concatenate(per_r, axis=1))
        res = _dot(jnp.concatenate(lhs_h, axis=0), w_ref[s])
        for h in range(NSA_KV_HEADS):
            o_ref[s, h] = res[h * nch:(h + 1) * nch]


def _chunk_proj(pages_arr, layer, page_table, w1c, n_groups):
    nch = N_PAGES * PAGE_SIZE // CMP_STRIDE

    def page_spec(pg):
        return pl.BlockSpec((None, None, PAGE_SIZE, NSA_KV_HEADS, 2, HEAD_DIM),
                            lambda b, pt: (layer, pt[b * N_PAGES + pg], 0, 0, 0, 0))

    in_specs = [page_spec(p) for p in range(N_PAGES)]
    in_specs.append(pl.BlockSpec((2, CMP_STRIDE * HEAD_DIM, 2 * HEAD_DIM), lambda b, pt: (0, 0, 0)))
    gs = pltpu.PrefetchScalarGridSpec(
        num_scalar_prefetch=1, grid=(n_groups,), in_specs=in_specs,
        out_specs=pl.BlockSpec((2, NSA_KV_HEADS, nch, 2 * HEAD_DIM), lambda b, pt: (0, 0, b, 0)))
    return pl.pallas_call(
        _chunk_proj_kernel, grid_spec=gs,
        out_shape=jax.ShapeDtypeStruct((2, NSA_KV_HEADS, n_groups * nch, 2 * HEAD_DIM), F32),
        compiler_params=_cp("parallel"), name="cmp_chunk_proj")(page_table, *([pages_arr] * N_PAGES), w1c)


def _cmp_bias_kernel(pe_ref, w_ref, b_ref, o_ref):
    o_ref[...] = b_ref[...] + _dot(pe_ref[...], w_ref[...])


def _cmp_bias(cmp_pe, cmp_w1, cmp_b1):
    k = 2 * CMP_STRIDE * HEAD_DIM
    pe = jnp.broadcast_to(cmp_pe.reshape(2, 1, k), (2, 8, k)).astype(BF16)
    w = cmp_w1.reshape(2, k, HEAD_DIM).astype(BF16)
    b = jnp.broadcast_to(cmp_b1.reshape(2, 1, HEAD_DIM), (2, 8, HEAD_DIM))
    return pl.pallas_call(
        _cmp_bias_kernel, grid=(2,),
        in_specs=[pl.BlockSpec((None, 8, k), lambda s: (s, 0, 0)),
                  pl.BlockSpec((None, k, HEAD_DIM), lambda s: (s, 0, 0)),
                  pl.BlockSpec((None, 8, HEAD_DIM), lambda s: (s, 0, 0))],
        out_specs=pl.BlockSpec((None, 8, HEAD_DIM), lambda s: (s, 0, 0)),
        out_shape=jax.ShapeDtypeStruct((2, 8, HEAD_DIM), F32),
        compiler_params=_cp("parallel"), name="cmp_bias")(pe, w, b)


def _cmp_mlp_kernel(p_ref, b_ref, w2_ref, o_ref):
    n = p_ref.shape[2]
    for s in range(2):
        for h in range(NSA_KV_HEADS):
            p = p_ref[s, h]
            nxt = pltpu.roll(p[:, HEAD_DIM:], n - 1, axis=0)
            hid = _silu(p[:, :HEAD_DIM] + nxt + b_ref[s, 0:1, :])
            o_ref[h * 2 + s] = _dot(hid.astype(BF16), w2_ref[s]).astype(o_ref.dtype)


def _cmp_mlp(p, b1c, w2, n_groups, out_dtype):
    nch = p.shape[2] // n_groups
    return pl.pallas_call(
        _cmp_mlp_kernel, grid=(n_groups,),
        in_specs=[pl.BlockSpec((2, NSA_KV_HEADS, nch, 2 * HEAD_DIM), lambda b: (0, 0, b, 0)),
                  pl.BlockSpec((2, 8, HEAD_DIM), lambda b: (0, 0, 0)),
                  pl.BlockSpec((2, HEAD_DIM, HEAD_DIM), lambda b: (0, 0, 0))],
        out_specs=pl.BlockSpec((None, 2 * NSA_KV_HEADS, nch, HEAD_DIM), lambda b: (b, 0, 0, 0)),
        out_shape=jax.ShapeDtypeStruct((n_groups, 2 * NSA_KV_HEADS, nch, HEAD_DIM), out_dtype),
        compiler_params=_cp("parallel"), name="cmp_mlp")(p, b1c, w2)


def _bucket_np(dist):
    n = np.maximum(dist, 0)
    exact = REL_BUCKETS // 2
    nf = np.maximum(n, 1).astype(np.float32)
    big = exact + (np.log(nf / np.float32(exact)) / np.float32(math.log(REL_MAX_DIST / exact))
                   * np.float32(REL_BUCKETS - exact)).astype(np.int32)
    return np.where(n < exact, n, np.minimum(big, REL_BUCKETS - 1))


def _bias_table(rel_bias_t, dist, valid):
    tab = rel_bias_t[:, _bucket_np(dist)]
    return jnp.where(jnp.asarray(valid), tab, NEG)


CMP_PAD = 120
CMP_ROWS = 768


def _wsel_np(n_cmp, n_blocks, rows, pad):
    w = np.zeros((rows, 128), np.float32)
    c0 = np.arange(n_cmp)[:, None] * CMP_STRIDE
    s0 = np.arange(n_blocks)[None, :] * SEL_BLOCK
    inter = np.minimum(c0 + 2 * CMP_STRIDE, s0 + SEL_BLOCK) - np.maximum(c0, s0)
    w[pad:pad + n_cmp, :n_blocks] = np.maximum(inter, 0).astype(np.float32) / (2 * CMP_STRIDE)
    return w


M0 = -1e20


def _nsa_prompt_kernel(q_ref, kcp_ref, vcp_ref, wselw_ref, kco_ref, vco_ref, wselo_ref,
                       ks_ref, vs_ref, kw_ref, vw_ref, small_ref, tc_ref, td_ref, tp_ref, cb_ref,
                       o_ref, m_scr, l_scr, acc_scr, imp_scr, out_scr, sel_scr, gate_scr):
    g = pl.program_id(0)
    i = pl.program_id(1)
    qb = Q_BLOCK
    w = Q_PER_KV * qb
    qf = q_ref[...]
    qt = jnp.concatenate([qf[:, j * HEAD_DIM:(j + 1) * HEAD_DIM].T.astype(BF16) for j in range(Q_PER_KV)],
                         axis=1)
    row = lax.broadcasted_iota(jnp.int32, (qb, qb), 0)
    col = lax.broadcasted_iota(jnp.int32, (qb, qb), 1)
    row4 = lax.broadcasted_iota(jnp.int32, (qb, w), 0)
    cb = cb_ref[...]
    tile4 = lambda x: jnp.concatenate([x] * Q_PER_KV, axis=1)

    gate_scr[...] = jax.nn.sigmoid(small_ref[...]).T

    def gate_row(branch):
        start = GATE_LANE0 + branch * NSA_HEADS + g * Q_PER_KV
        r8 = gate_scr[pl.ds(pl.multiple_of((start // 8) * 8, 8), 8), :]
        r4 = jnp.where(start % 8 == 0, r8[0:Q_PER_KV], r8[Q_PER_KV:2 * Q_PER_KV])
        return jnp.concatenate([r4[j:j + 1, :] for j in range(Q_PER_KV)], axis=1)

    def reset():
        m_scr[...] = jnp.full(m_scr.shape, M0, F32)
        l_scr[...] = jnp.zeros(l_scr.shape, F32)
        acc_scr[...] = jnp.zeros(acc_scr.shape, F32)

    def flash(parts, wsel_t=None):
        ss = []
        for k, _, bias, mask in parts:
            s = _dot(k, qt) * (SCALE * LOG2E) + bias
            ss.append(s if mask is None else jnp.where(mask, s, NEG))
        m_prev = m_scr[...]
        m_new = m_prev
        for s in ss:
            m_new = jnp.maximum(m_new, jnp.max(s, axis=0, keepdims=True))
        alpha = jnp.exp2(m_prev - m_new)
        ps = [jnp.exp2(s - m_new) for s in ss]
        lsum = jnp.sum(ps[0], axis=0, keepdims=True)
        for p in ps[1:]:
            lsum = lsum + jnp.sum(p, axis=0, keepdims=True)
        l_scr[...] = alpha * l_scr[...] + lsum
        m_scr[...] = m_new
        pb = jnp.concatenate([p.astype(BF16) for p in ps], axis=0)
        vt = jnp.concatenate([part[1] for part in parts], axis=1)
        acc_scr[...] = acc_scr[...] * alpha + _dot(vt, pb)
        if wsel_t is not None:
            imp_scr[...] = imp_scr[...] * alpha + _dot(wsel_t, pb)

    def normalized(ref):
        return ref[...] / jnp.maximum(l_scr[...], 1e-30)

    def lane_cat(ref, n):
        return jnp.concatenate([ref[u] for u in range(n)], axis=1)

    reset()
    imp_scr[...] = jnp.zeros(imp_scr.shape, F32)
    per_q = qb // CMP_STRIDE
    npos = lax.broadcasted_iota(jnp.int32, (4 * qb, w), 0)
    d0 = pl.multiple_of(i * per_q, 8)
    kcw = kcp_ref[pl.ds(d0, qb), :].astype(BF16)
    vcw = vcp_ref[pl.ds(d0, qb), :].T.astype(BF16)
    flash([(kco_ref[...].reshape(4 * qb, HEAD_DIM), lane_cat(vco_ref, 4), cb,
            (npos >= CMP_PAD) & (npos < i * per_q)),
           (kcw, vcw, tc_ref[...], (row4 + i * per_q) >= CMP_PAD)],
          jnp.concatenate([lane_cat(wselo_ref, 4), wselw_ref[...]], axis=1))
    out_scr[...] = gate_row(0) * normalized(acc_scr)
    impn = normalized(imp_scr)
    imp = impn[:, 0:qb]
    for j in range(1, Q_PER_KV):
        imp = imp + impn[:, j * qb:(j + 1) * qb]

    rowf = row.astype(F32)
    cur = 2 * i + (col >= SEL_BLOCK).astype(jnp.int32)
    forced = (row == 0) | (row == cur) | (row == cur - 1)
    score = jnp.where(forced, FORCED_SCORE, jnp.where(row <= cur, imp, -1.0))
    sel = jnp.zeros((qb, qb), F32)
    for _ in range(N_SEL):
        mx = jnp.max(score, axis=0, keepdims=True)
        idx = jnp.min(jnp.where(score == mx, rowf, 1e9), axis=0, keepdims=True)
        hit = rowf == idx
        sel = jnp.where(hit, 1.0, sel)
        score = jnp.where(hit, -3e38, score)
    sel_scr[...] = sel

    def block_rows(r, n_blocks):
        rows = [jnp.broadcast_to(r[u:u + 1, :], (SEL_BLOCK, qb)) for u in range(n_blocks)]
        return tile4(jnp.concatenate(rows, axis=0)) > 0.5

    reset()
    gt = 4
    n_full = jnp.maximum(i - 1, 0) // gt

    def sel_body(gi, carry):
        t0 = gi * gt
        r8 = sel_scr[pl.ds(pl.multiple_of(gi * 2 * gt, 8), 2 * gt), :]
        k = ks_ref[pl.ds(t0, gt)].reshape(gt * qb, HEAD_DIM)
        vt = jnp.concatenate([vs_ref[t0 + u] for u in range(gt)], axis=1)
        flash([(k, vt, cb, block_rows(r8, 2 * gt))])
        return carry

    lax.fori_loop(0, n_full, sel_body, 0)

    t0 = n_full * gt
    last = pl.num_programs(1) - 1
    r16 = jnp.concatenate(
        [sel_scr[pl.ds(pl.multiple_of(t0 * 2, 8), 8), :],
         sel_scr[pl.ds(pl.multiple_of(jnp.minimum(t0 * 2 + 8, qb - 8), 8), 8), :]], axis=0)
    tdv = td_ref[...]
    tpv = tp_ref[...]
    parts = []
    for u in range(gt + 1):
        kt = t0 + u
        ktc = jnp.minimum(kt, last)
        bias = jnp.where(kt == i, tdv, jnp.where(kt == i - 1, tpv, cb))
        mask = block_rows(r16[2 * u:2 * u + 2], 2) & (kt <= i)
        parts.append((ks_ref[ktc], vs_ref[ktc], bias, mask))
    flash(parts)
    out_scr[...] = out_scr[...] + gate_row(1) * normalized(acc_scr)

    reset()
    upper = tile4((row > col).astype(F32)) > 0.5
    n_back = WINDOW // qb
    parts = []
    for u in range(n_back + 1):
        kt = i - n_back + u
        ktc = jnp.maximum(kt, 0)
        bias = tdv if u == n_back else (tpv if u == n_back - 1 else cb)
        mask = (upper & (kt >= 0)) if u == 0 else jnp.broadcast_to(kt >= 0, (qb, w))
        parts.append((kw_ref[ktc], vw_ref[ktc], bias, mask))
    flash(parts)
    out = out_scr[...] + gate_row(2) * normalized(acc_scr)
    for j in range(Q_PER_KV):
        o_ref[:, j * HEAD_DIM:(j + 1) * HEAD_DIM] = out[:, j * qb:(j + 1) * qb].T.astype(o_ref.dtype)


def _nsa_prompt(proj, kcv, kv_sel, kv_win, rel_bias, seq):
    qb = Q_BLOCK
    nb = seq // qb
    hk = NSA_KV_HEADS
    w = Q_PER_KV * qb

    def key_tiles(x):
        x5 = x.astype(BF16).reshape(nb, qb, hk, 2, HEAD_DIM)
        return x5[:, :, :, 0, :].transpose(2, 0, 1, 3), x5[:, :, :, 1, :].transpose(2, 0, 3, 1)

    ks, vs = key_tiles(kv_sel)
    kw, vw = key_tiles(kv_win)

    nck = kcv.shape[1]
    padded = jnp.pad(kcv, ((0, 0), (CMP_PAD, CMP_ROWS - CMP_PAD - nck), (0, 0)))
    kc_pad, vc_pad = padded[0::2], padded[1::2]
    n_old = 4 * qb
    kco = kc_pad[:, :n_old].astype(BF16).reshape(hk, 4, qb, HEAD_DIM)
    vco = vc_pad[:, :n_old].astype(BF16).reshape(hk, 4, qb, HEAD_DIM).transpose(0, 1, 3, 2)
    widx = (qb // CMP_STRIDE) * np.arange(nb)[:, None] + np.arange(qb)[None, :]
    wsel = _wsel_np(seq // CMP_STRIDE - 1, seq // SEL_BLOCK, CMP_ROWS, CMP_PAD)
    wselo = jnp.asarray(wsel[:n_old].reshape(4, qb, 128).transpose(0, 2, 1), BF16)
    wselw = jnp.asarray(wsel[widx].transpose(0, 2, 1), BF16)

    rbt = rel_bias.T
    a = np.arange(qb)[:, None]
    b = np.arange(qb)[None, :]
    lanes = lambda t16: (t16 * LOG2E).reshape(hk, Q_PER_KV, qb, qb).transpose(0, 3, 1, 2).reshape(hk, qb, w)
    td = lanes(_bias_table(rbt, a - b, a >= b))
    tp = lanes(_bias_table(rbt, a - b + qb, np.ones((qb, qb), bool)))
    dist_c = a - CMP_STRIDE * b + (CMP_STRIDE * CMP_PAD - 2 * CMP_STRIDE + 1)
    tc = lanes(_bias_table(rbt, dist_c, dist_c >= 0))
    cb = jnp.repeat(rel_bias[REL_BUCKETS - 1].reshape(hk, Q_PER_KV) * LOG2E, qb, axis=1).reshape(hk, 1, w)

    per_g = lambda shape: pl.BlockSpec((None,) + shape, lambda g, i: (g,) + (0,) * len(shape))
    return pl.pallas_call(
        _nsa_prompt_kernel, grid=(hk, nb),
        in_specs=[pl.BlockSpec((qb, w), lambda g, i: (i, C_Q // w + g)),
                  per_g((CMP_ROWS, HEAD_DIM)), per_g((CMP_ROWS, HEAD_DIM)),
                  pl.BlockSpec((None, qb, qb), lambda g, i: (i, 0, 0)),
                  per_g((4, qb, qb)), per_g((4, qb, qb)), pl.BlockSpec((4, qb, qb), lambda g, i: (0, 0, 0)),
                  per_g((nb, qb, qb)), per_g((nb, qb, qb)), per_g((nb, qb, qb)), per_g((nb, qb, qb)),
                  pl.BlockSpec((qb, 256), lambda g, i: (i, C_SMALL // 256)),
                  per_g((qb, w)), per_g((qb, w)), per_g((qb, w)), per_g((1, w))],
        out_specs=pl.BlockSpec((qb, w), lambda g, i: (i, g)),
        out_shape=jax.ShapeDtypeStruct((seq, NSA_HEADS * HEAD_DIM), BF16),
        scratch_shapes=[pltpu.VMEM((1, w), F32), pltpu.VMEM((1, w), F32), pltpu.VMEM((HEAD_DIM, w), F32),
                        pltpu.VMEM((qb, w), F32), pltpu.VMEM((HEAD_DIM, w), F32), pltpu.VMEM((qb, qb), F32),
                        pltpu.VMEM((256, qb), F32)],
        compiler_params=_cp("parallel", "arbitrary"), name="nsa_prompt")(
            proj, kc_pad, vc_pad, wselw, kco, vco, wselo, ks, vs, kw, vw, proj, tc, td, tp, cb)


N_BLK_S = -(-(PAST_LEN + 1) // SEL_BLOCK)
N_OLD_SEL = N_SEL - 1


def _nsa_sample_cmp_kernel(q_ref, kcv_ref, bias_ref, wsel_ref, oc_ref, idx_ref):
    hrow = lax.broadcasted_iota(jnp.int32, (NSA_HEADS, 128), 0) // Q_PER_KV
    psums = []
    bias = bias_ref[...]
    valid = bias > 0.5 * NEG
    for bb in range(q_ref.shape[0]):
        q = q_ref[bb].astype(BF16)
        o_c = jnp.zeros((NSA_HEADS, HEAD_DIM), F32)
        for g in range(NSA_KV_HEADS):
            s = jnp.where(valid, _dot_nt(q, kcv_ref[bb, 2 * g]) * SCALE + bias, NEG)
            mx = jnp.max(s, axis=-1, keepdims=True)
            e = jnp.where(valid, jnp.exp(s - mx), 0.0)
            p = e / jnp.maximum(jnp.sum(e, axis=-1, keepdims=True), 1e-30)
            mine = hrow == g
            o_c = o_c + jnp.where(mine, _dot(p.astype(BF16), kcv_ref[bb, 2 * g + 1]), 0.0)
            psums.append(jnp.sum(jnp.where(mine, p, 0.0), axis=0, keepdims=True))
        oc_ref[bb] = o_c
    psum = jnp.concatenate(psums, axis=0)
    imp = _dot(psum.astype(BF16), wsel_ref[...])
    lane = lax.broadcasted_iota(jnp.int32, imp.shape, 1)
    cur = N_BLK_S - 1
    forced = (lane == 0) | (lane == cur - 1)
    score = jnp.where(lane >= cur, -3e38, jnp.where(forced, FORCED_SCORE, imp))
    lanef = lane.astype(F32)
    out = jnp.zeros(imp.shape, F32)
    for r in range(N_OLD_SEL):
        mx = jnp.max(score, axis=-1, keepdims=True)
        idx = jnp.min(jnp.where(score == mx, lanef, 1e9), axis=-1, keepdims=True)
        out = jnp.where(lane == r, idx, out)
        score = jnp.where(lanef == idx, -3e38, score)
    idx_ref[...] = out.astype(jnp.int32)


def _nsa_sample_cmp(q_s, kcv_s, rel_bias):
    b = DEC_BATCH
    rbt = rel_bias.T
    n = np.arange(128)[None, :]
    dist = PAST_LEN - (CMP_STRIDE * n + 2 * CMP_STRIDE - 1)
    bias = _bias_table(rbt, dist, n < (PAST_LEN // CMP_STRIDE - 1))[:, 0, :]
    wsel = jnp.asarray(_wsel_np(PAST_LEN // CMP_STRIDE - 1, N_BLK_S, 128, 0), BF16)
    bb = 8
    return pl.pallas_call(
        _nsa_sample_cmp_kernel, grid=(b // bb,),
        in_specs=[pl.BlockSpec((bb, NSA_HEADS, HEAD_DIM), lambda i: (i, 0, 0)),
                  pl.BlockSpec((bb, 2 * NSA_KV_HEADS, 128, HEAD_DIM), lambda i: (i, 0, 0, 0)),
                  pl.BlockSpec((NSA_HEADS, 128), lambda i: (0, 0)),
                  pl.BlockSpec((128, 128), lambda i: (0, 0))],
        out_specs=[pl.BlockSpec((bb, NSA_HEADS, HEAD_DIM), lambda i: (i, 0, 0)),
                   pl.BlockSpec((bb * NSA_KV_HEADS, 128), lambda i: (i, 0))],
        out_shape=[jax.ShapeDtypeStruct((b, NSA_HEADS, HEAD_DIM), F32),
                   jax.ShapeDtypeStruct((b * NSA_KV_HEADS, 128), jnp.int32)],
        compiler_params=_cp("parallel"), name="nsa_sample_cmp")(q_s, kcv_s, bias, wsel)


def _nsa_sample_attn_kernel(pt_ref, idx_ref, *refs):
    del pt_ref
    blocks = refs[:N_OLD_SEL]
    (win_ref, news_ref, neww_ref, q_ref, oc_ref, gn_ref, bs_ref, bw_ref, b0_ref, o_ref) = refs[N_OLD_SEL:]
    b = pl.program_id(0)
    g = pl.program_id(1)
    q = q_ref[...].astype(BF16)
    gates = jax.nn.sigmoid(gn_ref[...])
    b0 = b0_ref[...]

    def attend(pieces, new_ref):
        new = new_ref[...]
        s_new = jnp.sum(q.astype(F32) * new[:, :HEAD_DIM].astype(BF16).astype(F32), axis=-1, keepdims=True)
        s_new = s_new * SCALE + b0
        mx = s_new
        for s, valid, _ in pieces:
            mx = jnp.maximum(mx, jnp.max(s, axis=-1, keepdims=True))
        e_new = jnp.exp(s_new - mx)
        den = e_new
        acc = e_new.astype(BF16).astype(F32) * new[:, HEAD_DIM:].astype(BF16).astype(F32)
        for s, valid, v in pieces:
            e = jnp.exp(s - mx)
            if valid is not None:
                e = jnp.where(valid, e, 0.0)
            den = den + jnp.sum(e, axis=-1, keepdims=True)
            acc = acc + _dot(e.astype(BF16), v)
        return acc / jnp.maximum(den, 1e-30)

    def interleaved(ref, rows):
        kv = ref[...].reshape(2 * rows, HEAD_DIM)
        return kv.astype(BF16), pltpu.roll(kv, 2 * rows - 1, axis=0).astype(BF16)

    pieces = []
    for kk in range(N_OLD_SEL):
        kvb, vsb = interleaved(blocks[kk], SEL_BLOCK)
        m = idx_ref[(b * NSA_KV_HEADS + g) * N_OLD_SEL + kk]
        s = _dot_nt(q, kvb) * SCALE + bs_ref[m]
        pieces.append((s, None, vsb))
    o_s = attend(pieces, news_ref)

    kvb, vsb = interleaved(win_ref, WINDOW)
    bw = bw_ref[...]
    valid_w = bw > 0.5 * NEG
    s_w = jnp.where(valid_w, _dot_nt(q, kvb) * SCALE + bw, NEG)
    o_w = attend([(s_w, valid_w, vsb)], neww_ref)

    out = gates[:, 0:1] * oc_ref[...] + gates[:, 1:2] * o_s + gates[:, 2:3] * o_w
    o_ref[...] = out.astype(o_ref.dtype)


def _nsa_sample_attn(layer, cache_sel, cache_win, page_table, idx, kvs_new, kvw_new, q_s, o_c, gn_s, rel_bias):
    b = DEC_BATCH
    hk = NSA_KV_HEADS
    rbt = rel_bias.T
    m = np.arange(N_BLK_S - 1)[:, None]
    r = np.arange(SEL_BLOCK)[None, :]
    bs = _bias_table(rbt, PAST_LEN - (SEL_BLOCK * m + r), np.ones((N_BLK_S - 1, SEL_BLOCK), bool))
    bs = bs.reshape(hk, Q_PER_KV, N_BLK_S - 1, SEL_BLOCK).transpose(0, 2, 1, 3)
    odd_masked = lambda t: jnp.stack([t, jnp.full(t.shape, NEG, F32)], axis=-1).reshape(t.shape[:-1] + (-1,))
    bs = odd_masked(bs)
    rw = np.arange(WINDOW)[None, :]
    bw = odd_masked(_bias_table(rbt, WINDOW - rw, rw >= 1)[:, 0, :].reshape(hk, Q_PER_KV, WINDOW))
    b0 = rel_bias[0].reshape(hk, Q_PER_KV, 1)
    per_page = PAGE_SIZE // SEL_BLOCK
    csel = cache_sel.reshape(DEPTH, -1, per_page, SEL_BLOCK, hk, 2, HEAD_DIM)

    def sel_spec(kk):
        def imap(bi, g, pt, ix):
            blk = ix[(bi * hk + g) * N_OLD_SEL + kk]
            return (layer, pt[bi * N_PAGES + blk // per_page], blk % per_page, 0, g, 0, 0)
        return pl.BlockSpec((None, None, None, SEL_BLOCK, None, 2, HEAD_DIM), imap)

    bgj = lambda last: pl.BlockSpec((None, None, Q_PER_KV, last), lambda bi, g, pt, ix: (bi, g, 0, 0))
    new_spec = pl.BlockSpec((None, None, 1, 2 * HEAD_DIM), lambda bi, g, pt, ix: (bi, g, 0, 0))
    in_specs = [sel_spec(kk) for kk in range(N_OLD_SEL)] + [
        pl.BlockSpec((None, None, WINDOW, None, 2, HEAD_DIM), lambda bi, g, pt, ix: (layer, bi, 0, g, 0, 0)),
        new_spec, new_spec, bgj(HEAD_DIM), bgj(HEAD_DIM), bgj(3),
        pl.BlockSpec((None, N_BLK_S - 1, Q_PER_KV, 2 * SEL_BLOCK), lambda bi, g, pt, ix: (g, 0, 0, 0)),
        pl.BlockSpec((None, Q_PER_KV, 2 * WINDOW), lambda bi, g, pt, ix: (g, 0, 0)),
        pl.BlockSpec((None, Q_PER_KV, 1), lambda bi, g, pt, ix: (g, 0, 0))]
    gs = pltpu.PrefetchScalarGridSpec(
        num_scalar_prefetch=2, grid=(b, hk), in_specs=in_specs,
        out_specs=bgj(HEAD_DIM))
    return pl.pallas_call(
        _nsa_sample_attn_kernel, grid_spec=gs,
        out_shape=jax.ShapeDtypeStruct((b, hk, Q_PER_KV, HEAD_DIM), F32),
        compiler_params=_cp("parallel", "parallel"), name="nsa_sample_attn")(
            page_table.reshape(-1), idx.reshape(-1), *([csel] * N_OLD_SEL), cache_win,
            kvs_new.reshape(b, hk, 1, 2 * HEAD_DIM), kvw_new.reshape(b, hk, 1, 2 * HEAD_DIM),
            q_s.reshape(b, hk, Q_PER_KV, HEAD_DIM), o_c.reshape(b, hk, Q_PER_KV, HEAD_DIM), gn_s, bs, bw, b0)


def _router_kernel(h_ref, w_ref, b_ref, o_ref):
    logits = _dot(h_ref[...], w_ref[...]) + b_ref[...]
    lane = lax.broadcasted_iota(jnp.int32, logits.shape, 1)
    lanef = lane.astype(F32)

    def softmax(mask):
        s = jnp.where(mask, logits, NEG)
        e = jnp.where(mask, jnp.exp(s - jnp.max(s, axis=-1, keepdims=True)), 0.0)
        return e / jnp.sum(e, axis=-1, keepdims=True)

    def top1(p, mask):
        mx = jnp.max(jnp.where(mask, p, -1.0), axis=-1, keepdims=True)
        idx = jnp.min(jnp.where(mask & (p == mx), lanef, 1e9), axis=-1, keepdims=True)
        return mx, idx

    gmask = (lane >= N_EXPERTS) & (lane < N_EXPERTS + MOE_GROUPS)
    pg = softmax(gmask)
    p_top, g_idx = top1(pg, gmask)
    e_lo = (g_idx - N_EXPERTS) * MOE_PER_GROUP
    emask = (lanef >= e_lo) & (lanef < e_lo + MOE_PER_GROUP)
    pe = softmax(emask)
    w_a, i_a = top1(pe, emask)
    mask_b = emask & (lanef != i_a)
    w_b, i_b = top1(pe, mask_b)
    tot = w_a + w_b
    gate = jnp.where(lanef == i_a, w_a / tot * p_top, jnp.where(lanef == i_b, w_b / tot * p_top, 0.0))
    o_ref[...] = gate


def _router(hn, w_group, b_group, w_exp, b_exp):
    w = jnp.concatenate([w_exp, w_group, jnp.zeros((D_MODEL, 128 - N_EXPERTS - MOE_GROUPS), F32)], axis=1)
    bias = jnp.concatenate([b_exp, b_group, jnp.zeros((128 - N_EXPERTS - MOE_GROUPS,), F32)]).reshape(1, 128)
    return pl.pallas_call(
        _router_kernel, grid=(M_ALL // TM,),
        in_specs=[pl.BlockSpec((TM, D_MODEL), lambda i: (i, 0)),
                  pl.BlockSpec((D_MODEL, 128), lambda i: (0, 0)),
                  pl.BlockSpec((1, 128), lambda i: (0, 0))],
        out_specs=pl.BlockSpec((TM, 128), lambda i: (i, 0)),
        out_shape=jax.ShapeDtypeStruct((M_ALL, 128), F32),
        compiler_params=_cp("parallel"), name="moe_router")(hn, w.astype(BF16), bias)


def _moe_kernel(h_ref, win_ref, wout_ref, gate_ref, x_ref, o_ref, acc_scr):
    e = pl.program_id(1)

    @pl.when(e == 0)
    def _():
        acc_scr[...] = jnp.zeros(acc_scr.shape, F32)

    u = _dot(h_ref[...], win_ref[...])
    gate = gate_ref[...]
    lane = lax.broadcasted_iota(jnp.int32, gate.shape, 1)
    ge = jnp.sum(jnp.where(lane == e, gate, 0.0), axis=-1, keepdims=True)
    act = _silu(u[:, :D_EXPERT]) * u[:, D_EXPERT:] * ge
    acc_scr[...] = acc_scr[...] + _dot(act.astype(BF16), wout_ref[...])

    @pl.when(e == N_EXPERTS - 1)
    def _():
        o_ref[...] = x_ref[...] + acc_scr[...]


def _moe(hn, gate, x, w_in_e, w_out_e):
    return pl.pallas_call(
        _moe_kernel, grid=(M_ALL // TM, N_EXPERTS),
        in_specs=[pl.BlockSpec((TM, D_MODEL), lambda i, e: (i, 0)),
                  pl.BlockSpec((None, D_MODEL, 2 * D_EXPERT), lambda i, e: (e, 0, 0)),
                  pl.BlockSpec((None, D_EXPERT, D_MODEL), lambda i, e: (e, 0, 0)),
                  pl.BlockSpec((TM, 128), lambda i, e: (i, 0)),
                  pl.BlockSpec((TM, D_MODEL), lambda i, e: (i, 0))],
        out_specs=pl.BlockSpec((TM, D_MODEL), lambda i, e: (i, 0)),
        out_shape=jax.ShapeDtypeStruct((M_ALL, D_MODEL), F32),
        scratch_shapes=[pltpu.VMEM((TM, D_MODEL), F32)],
        compiler_params=_cp("parallel", "arbitrary"), name="moe_experts")(hn, w_in_e, w_out_e, gate, x)


def _prep_w_in(w):
    offs = np.cumsum((0,) + IN_SIZES)
    z, xbc, dt, q, kvc, kvs, kvw, gn, gm = [w[:, offs[k]:offs[k + 1]] for k in range(len(IN_SIZES))]
    small = jnp.concatenate([dt, gn, jnp.zeros((D_MODEL, 256 - dt.shape[1] - gn.shape[1]), w.dtype)], axis=1)
    return jnp.concatenate([z, xbc, q, kvc, kvs, kvw, gm, small], axis=1).astype(BF16)


def _win_out_kernel(cache_ref, new_ref, o_ref, sem):
    step = DEC_BATCH // WIN_COPY_CHUNKS
    copies = []
    for l in range(DEPTH):
        for c in range(WIN_COPY_CHUNKS):
            bs = pl.ds(c * step, step)
            copies.append(pltpu.make_async_copy(cache_ref.at[l, bs, pl.ds(1, WINDOW - 1)],
                                                o_ref.at[l, bs, pl.ds(0, WINDOW - 1)], sem.at[len(copies)]))
        copies.append(pltpu.make_async_copy(new_ref.at[l], o_ref.at[l, :, pl.ds(WINDOW - 1, 1)],
                                            sem.at[len(copies)]))
    for cp in copies:
        cp.start()
    for cp in copies:
        cp.wait()


WIN_COPY_CHUNKS = 4


def _win_out(cache_win_kv, new_rows):
    any_spec = pl.BlockSpec(memory_space=pl.ANY)
    return pl.pallas_call(
        _win_out_kernel, in_specs=[any_spec, any_spec], out_specs=any_spec,
        out_shape=jax.ShapeDtypeStruct(cache_win_kv.shape, cache_win_kv.dtype),
        scratch_shapes=[pltpu.SemaphoreType.DMA((DEPTH * (WIN_COPY_CHUNKS + 1),))],
        name="win_out")(cache_win_kv, new_rows)


def _layer(l, x, p, cache_cmp_kv, cache_sel_kv, cache_win_kv, state_t, h_acc, state_conv_l, page_table, rel_bias):
    s = SEQ
    b = DEC_BATCH
    h1 = _rmsnorm(x, p['norm1_w'], BF16)
    proj = _mm(h1, _prep_w_in(p['w_in']), tn=1792, out_dtype=F32, name="in_proj")

    kv_new = {name: proj[:, c:c + KV_COLS] for name, c in (('cmp', C_KVC), ('sel', C_KVS), ('win', C_KVW))}

    act_p = _conv_prompt(proj, p['conv_w'], p['conv_b'])
    bt = act_p[:, D_SSD:D_SSD + SSD_GROUPS * SSD_STATE].T.astype(BF16)
    dtr = proj[:, C_SMALL:C_SMALL + SSD_HEADS]
    y_p, ssm_p = _ssd_scan_prompt(act_p, bt, dtr[:s], p['dt_bias'], p['a_log'], p['d_skip'])
    conv_p = proj[s - (SSD_CONV - 1):s, C_XBC:C_XBC + CONV_CH]

    xbc_s = proj[s:, C_XBC:C_XBC + CONV_CH]
    act_s = _conv_sample(state_conv_l.transpose(1, 0, 2), xbc_s, p['conv_w'], p['conv_b'])
    y_s, h_acc = _ssd_step_sample(state_t, l, h_acc, act_s, dtr[s:], p['dt_bias'], p['a_log'], p['d_skip'])
    conv_s = jnp.concatenate([state_conv_l[:, 1:], xbc_s[:, None, :]], axis=1)

    yn = _gated_norm(y_p, y_s.reshape(b, D_SSD), proj, p['ssd_norm_w'])
    y_a = _mm(yn, p['w_ssd_out'].astype(BF16), tn=1024, out_dtype=F32, name="ssd_out")

    b1c = _cmp_bias(p['cmp_pe'], p['cmp_w1'], p['cmp_b1'])
    w1c = (p['cmp_w1'].reshape(2, 2, CMP_STRIDE * HEAD_DIM, HEAD_DIM).transpose(0, 2, 1, 3)
           .reshape(2, CMP_STRIDE * HEAD_DIM, 2 * HEAD_DIM).astype(BF16))
    w2 = p['cmp_w2'].astype(BF16)

    n_grp_p = s // (N_PAGES * PAGE_SIZE)
    kvc_pages = kv_new['cmp'][:s].reshape(1, s // PAGE_SIZE, PAGE_SIZE, NSA_KV_HEADS, 2, HEAD_DIM)
    pt_p = jnp.arange(n_grp_p * N_PAGES, dtype=jnp.int32)
    pp = _chunk_proj(kvc_pages, 0, pt_p, w1c, n_grp_p)
    kcv_p = _cmp_mlp(pp, b1c, w2, 1, F32)[0]
    attn_p = _nsa_prompt(proj, kcv_p, kv_new['sel'][:s], kv_new['win'][:s], rel_bias, s)

    ps = _chunk_proj(cache_cmp_kv, l, page_table.reshape(-1), w1c, b)
    kcv_s = _cmp_mlp(ps, b1c, w2, b, BF16)
    q_s = proj[s:, C_Q:C_Q + NSA_HEADS * HEAD_DIM].reshape(b, NSA_HEADS, HEAD_DIM)
    o_c, idx = _nsa_sample_cmp(q_s, kcv_s, rel_bias)
    gn_s = (proj[s:, C_SMALL + GATE_LANE0:C_SMALL + GATE_LANE0 + 3 * NSA_HEADS]
            .reshape(b, 3, NSA_KV_HEADS, Q_PER_KV).transpose(0, 2, 3, 1))
    attn_s = _nsa_sample_attn(l, cache_sel_kv, cache_win_kv, page_table, idx[:, :N_OLD_SEL],
                              kv_new['sel'][s:], kv_new['win'][s:], q_s, o_c, gn_s, rel_bias)
    attn = jnp.concatenate([attn_p, attn_s.reshape(b, NSA_HEADS * HEAD_DIM).astype(BF16)], axis=0)

    tn = 1024
    m = _mm(attn, p['w_nsa_out'].astype(BF16), tn=tn, out_dtype=BF16, name="nsa_out_merge",
            extras=((y_a, 0), (proj, C_GM // tn), (proj, (C_GM + D_MODEL) // tn)), epilogue=_merge_epilogue)
    x = _mm(m, p['w_out'].astype(BF16), tn=tn, out_dtype=F32, name="out_proj", extras=((x, 0),),
            epilogue=_resid_epilogue)
    h2 = _rmsnorm(x, p['norm2_w'], BF16)
    gate = _router(h2, p['w_router_group'], p['b_router_group'], p['w_router_exp'], p['b_router_exp'])
    x = _moe(h2, gate, x, p['w_exp_in'].astype(BF16), p['w_exp_out'].astype(BF16))

    kv5 = lambda a: a.reshape(a.shape[0], NSA_KV_HEADS, 2, HEAD_DIM)
    outs = dict(
        cmp_p=kv5(kv_new['cmp'][:s])[None], cmp_s=kv5(kv_new['cmp'][s:])[:, None],
        sel_p=kv5(kv_new['sel'][:s])[None], sel_s=kv5(kv_new['sel'][s:])[:, None],
        win_p=kv5(kv_new['win'][s - WINDOW:s])[None], win_new=kv5(kv_new['win'][s:])[:, None],
        ssm_p=ssm_p[None], conv_p=conv_p[None], conv_s=conv_s)
    return x, h_acc, outs


def kernel(x_prompt, x_sample, cache_cmp_kv, cache_sel_kv, cache_win_kv, state_ssm, state_conv, page_table,
           rel_bias, norm1_w, norm2_w, final_norm_w, w_in, conv_w, conv_b, dt_bias, a_log, d_skip, ssd_norm_w,
           w_ssd_out, cmp_pe, cmp_w1, cmp_b1, cmp_w2, w_nsa_out, w_out, w_router_group, b_router_group,
           w_router_exp, b_router_exp, w_exp_in, w_exp_out):
    x = jnp.concatenate([x_prompt[0], x_sample[:, 0]], axis=0)
    state_t = state_ssm.reshape(DEPTH, DEC_BATCH, D_SSD, SSD_STATE).transpose(0, 1, 3, 2)
    h_acc = None
    per_layer = []
    for l in range(DEPTH):
        p = {'norm1_w': norm1_w[l], 'norm2_w': norm2_w[l], 'w_in': w_in[l], 'conv_w': conv_w[l],
             'conv_b': conv_b[l], 'dt_bias': dt_bias[l], 'a_log': a_log[l], 'd_skip': d_skip[l],
             'ssd_norm_w': ssd_norm_w[l], 'w_ssd_out': w_ssd_out[l], 'cmp_pe': cmp_pe[l], 'cmp_w1': cmp_w1[l],
             'cmp_b1': cmp_b1[l], 'cmp_w2': cmp_w2[l], 'w_nsa_out': w_nsa_out[l], 'w_out': w_out[l],
             'w_router_group': w_router_group[l], 'b_router_group': b_router_group[l],
             'w_router_exp': w_router_exp[l], 'b_router_exp': b_router_exp[l],
             'w_exp_in': w_exp_in[l], 'w_exp_out': w_exp_out[l]}
        x, h_acc, outs = _layer(l, x, p, cache_cmp_kv, cache_sel_kv, cache_win_kv, state_t, h_acc, state_conv[l],
                                page_table, rel_bias)
        per_layer.append(outs)
    y = _rmsnorm(x, final_norm_w, F32)
    stack = lambda k: jnp.stack([o[k] for o in per_layer])
    ssm_s = h_acc.transpose(0, 1, 3, 2).reshape(DEPTH, DEC_BATCH, SSD_HEADS, SSD_HEADDIM, SSD_STATE)
    win_s = _win_out(cache_win_kv, stack('win_new'))
    return (y[:SEQ][None], y[SEQ:][:, None], stack('cmp_p'), stack('cmp_s'), stack('sel_p'), stack('sel_s'),
            stack('win_p'), win_s, stack('ssm_p'), ssm_s, stack('conv_p'), stack('conv_s'))
```

```python
import functools
import math

import numpy as np
import jax
import jax.numpy as jnp
from jax import lax
from jax.experimental import pallas as pl
from jax.experimental.pallas import tpu as pltpu

F32 = jnp.float32
BF16 = jnp.bfloat16
NEG = -1e30

D_MODEL = 2048
SEQ = 8192
DEPTH = 2
DEC_BATCH = 128
PAST_LEN = 2048
PAGE_SIZE = 128
N_PAGES = PAST_LEN // PAGE_SIZE
D_SSD = 4096
SSD_HEADDIM = 64
SSD_HEADS = 64
SSD_GROUPS = 8
SSD_STATE = 128
SSD_CONV = 4
SSD_CHUNK = 256
CONV_CH = 6144
NSA_HEADS = 16
NSA_KV_HEADS = 4
HEAD_DIM = 128
Q_PER_KV = 4
CMP_STRIDE = 16
SEL_BLOCK = 64
N_SEL = 16
WINDOW = 512
Q_BLOCK = 128
FORCED_SCORE = 1e4
REL_BUCKETS = 32
REL_MAX_DIST = 128
MOE_GROUPS = 4
MOE_PER_GROUP = 4
N_EXPERTS = 16
D_EXPERT = 512
RMS_EPS = 1e-6
KV_COLS = 1024
IN_SIZES = (D_SSD, CONV_CH, SSD_HEADS, NSA_HEADS * HEAD_DIM, KV_COLS, KV_COLS, KV_COLS, 3 * NSA_HEADS, 2 * D_MODEL)
SCALE = HEAD_DIM ** -0.5
LOG2E = math.log2(math.e)

C_Z = 0
C_XBC = 4096
C_Q = 10240
C_KVC = 12288
C_KVS = 13312
C_KVW = 14336
C_GM = 15360
C_SMALL = 19456
N_PROJ = 19712
GATE_LANE0 = 64

M_ALL = SEQ + DEC_BATCH
TM = 640
VMEM_LIMIT_MIB = 56


def _cp(*sem):
    return pltpu.CompilerParams(dimension_semantics=sem, vmem_limit_bytes=VMEM_LIMIT_MIB * 1024 * 1024)


def _dot(a, b):
    return jnp.dot(a, b, preferred_element_type=F32)


def _dot_nt(a, b):
    return lax.dot_general(a, b, (((1,), (1,)), ((), ())), preferred_element_type=F32)


def _dot_f32(a, b):
    return jnp.dot(a, b, preferred_element_type=F32, precision=lax.Precision.HIGHEST)


def _silu(x):
    return x * jax.nn.sigmoid(x)


def _softplus(x):
    return jnp.maximum(x, 0.0) + jnp.log1p(jnp.exp(-jnp.abs(x)))


def _split_hi_lo(v):
    hi = v.astype(BF16)
    lo = (v - hi.astype(F32)).astype(BF16)
    return hi, lo


def _expand(v, e):
    hi, lo = _split_hi_lo(v)
    return _dot(hi, e) + _dot(lo, e)


def _rmsnorm_kernel(x_ref, w_ref, o_ref):
    x = x_ref[...]
    ms = jnp.mean(x * x, axis=-1, keepdims=True)
    o_ref[...] = (x * lax.rsqrt(ms + RMS_EPS) * w_ref[...]).astype(o_ref.dtype)


def _rmsnorm(x, w, out_dtype, tm=TM):
    m, d = x.shape
    return pl.pallas_call(
        _rmsnorm_kernel, grid=(m // tm,),
        in_specs=[pl.BlockSpec((tm, d), lambda i: (i, 0)), pl.BlockSpec((1, d), lambda i: (0, 0))],
        out_specs=pl.BlockSpec((tm, d), lambda i: (i, 0)),
        out_shape=jax.ShapeDtypeStruct((m, d), out_dtype),
        compiler_params=_cp("parallel"), name="rmsnorm")(x, w.reshape(1, d))


def _gated_norm_kernel(yp_ref, ys_ref, z_ref, w_ref, o_ref, *, n_prompt_tiles):
    i = pl.program_id(0)
    y = jnp.where(i < n_prompt_tiles, yp_ref[...], ys_ref[...])
    gated = y * _silu(z_ref[...])
    ms = jnp.mean(gated * gated, axis=-1, keepdims=True)
    o_ref[...] = (gated * lax.rsqrt(ms + RMS_EPS) * w_ref[...]).astype(o_ref.dtype)


def _gated_norm(y_p, y_s, proj, w):
    tm = DEC_BATCH
    npt = SEQ // tm
    return pl.pallas_call(
        functools.partial(_gated_norm_kernel, n_prompt_tiles=npt), grid=(M_ALL // tm,),
        in_specs=[pl.BlockSpec((tm, D_SSD), lambda i: (jnp.minimum(i, npt - 1), 0)),
                  pl.BlockSpec((tm, D_SSD), lambda i: (0, 0)),
                  pl.BlockSpec((tm, D_SSD), lambda i: (i, C_Z // D_SSD)),
                  pl.BlockSpec((1, D_SSD), lambda i: (0, 0))],
        out_specs=pl.BlockSpec((tm, D_SSD), lambda i: (i, 0)),
        out_shape=jax.ShapeDtypeStruct((M_ALL, D_SSD), BF16),
        compiler_params=_cp("parallel"), name="gated_norm")(y_p, y_s, proj, w.reshape(1, D_SSD))


def _mm_kernel(a_ref, w_ref, *rest, epilogue):
    o_ref = rest[-1]
    acc = _dot(a_ref[...], w_ref[...])
    if epilogue is not None:
        acc = epilogue(acc, *[r[...] for r in rest[:-1]])
    o_ref[...] = acc.astype(o_ref.dtype)


def _mm(a, w, *, tn, out_dtype, name, extras=(), epilogue=None, tm=TM):
    m, k = a.shape
    n = w.shape[1]
    in_specs = [pl.BlockSpec((tm, k), lambda j, i: (i, 0)), pl.BlockSpec((k, tn), lambda j, i: (0, j))]
    args = [a, w]
    for arr, coff in extras:
        in_specs.append(pl.BlockSpec((tm, tn), lambda j, i, coff=coff: (i, coff + j)))
        args.append(arr)
    return pl.pallas_call(
        functools.partial(_mm_kernel, epilogue=epilogue), grid=(n // tn, m // tm),
        in_specs=in_specs, out_specs=pl.BlockSpec((tm, tn), lambda j, i: (i, j)),
        out_shape=jax.ShapeDtypeStruct((m, n), out_dtype),
        compiler_params=_cp("parallel", "parallel"), name=name)(*args)


def _merge_epilogue(y_b, y_a, g_a, g_b):
    return jax.nn.sigmoid(g_a) * y_a + jax.nn.sigmoid(g_b) * y_b


def _resid_epilogue(acc, x):
    return x + acc


CONV_TR = 512
CONV_TC = 512


def _conv_prompt_kernel(x_ref, halo_ref, w_ref, b_ref, o_ref):
    i = pl.program_id(1)
    halo = jnp.where(i > 0, halo_ref[...], 0.0)
    x = jnp.concatenate([halo, x_ref[...]], axis=0)
    n = x.shape[0]
    acc = b_ref[...] + x[8:] * w_ref[SSD_CONV - 1:SSD_CONV, :]
    for k in range(SSD_CONV - 1):
        shifted = pltpu.roll(x, SSD_CONV - 1 - k, axis=0)
        acc = acc + shifted[8:] * w_ref[k:k + 1, :]
    del n
    o_ref[...] = _silu(acc)


def _conv_prompt(proj, conv_w, conv_b):
    nct = CONV_CH // CONV_TC
    nrt = SEQ // CONV_TR
    c0 = C_XBC // CONV_TC
    return pl.pallas_call(
        _conv_prompt_kernel, grid=(nct, nrt),
        in_specs=[pl.BlockSpec((CONV_TR, CONV_TC), lambda c, i: (i, c0 + c)),
                  pl.BlockSpec((8, CONV_TC), lambda c, i: (jnp.maximum(i * (CONV_TR // 8) - 1, 0), c0 + c)),
                  pl.BlockSpec((SSD_CONV, CONV_TC), lambda c, i: (0, c)),
                  pl.BlockSpec((1, CONV_TC), lambda c, i: (0, c))],
        out_specs=pl.BlockSpec((CONV_TR, CONV_TC), lambda c, i: (i, c)),
        out_shape=jax.ShapeDtypeStruct((SEQ, CONV_CH), F32),
        compiler_params=_cp("parallel", "parallel"), name="conv_prompt")(
            proj, proj, conv_w, conv_b.reshape(1, CONV_CH))


GW = SSD_HEADS // SSD_GROUPS * SSD_HEADDIM
HPG = SSD_HEADS // SSD_GROUPS


def _scan_kernel(xs_ref, b_ref, c_ref, bt_ref, dtr_ref, dtrt_ref, dtb_r_ref, dtb_c_ref, al_r_ref, al_c_ref,
                 dskip_ref, e_ref, y_ref, hout_ref, h_scr):
    c = pl.program_id(1)
    q = SSD_CHUNK

    @pl.when(c == 0)
    def _():
        h_scr[...] = jnp.zeros((SSD_STATE, GW), F32)

    dt = _softplus(dtr_ref[...] + dtb_r_ref[...])
    d_a = dt * (-jnp.exp(al_r_ref[...]))
    dt_t = _softplus(dtrt_ref[...] + dtb_c_ref[...])
    d_a_t = dt_t * (-jnp.exp(al_c_ref[...]))
    row = lax.broadcasted_iota(jnp.int32, (q, q), 0)
    col = lax.broadcasted_iota(jnp.int32, (q, q), 1)
    tril = row >= col
    cum = _dot_f32(tril.astype(F32), d_a)
    cum_t = _dot_f32(d_a_t, (col >= row).astype(F32))
    cum_last = cum[q - 1:q, :]
    e = e_ref[...]
    dt_e = _expand(dt, e)
    ecum_e = _expand(jnp.exp(cum), e)
    tail_e = _expand(jnp.exp(cum_last - cum), e)
    elast_e = _expand(jnp.broadcast_to(jnp.exp(cum_last), (8, 128)), e)[0:1, :]

    xs = xs_ref[...]
    xdt = xs * dt_e
    xdt_b = xdt.astype(BF16)
    xw = (xdt * tail_e).astype(BF16)
    bg = b_ref[...].astype(BF16)
    cg = c_ref[...].astype(BF16)
    cb = _dot_nt(cg, bg)
    h_prev = h_scr[...]
    y = _dot(cg, h_prev.astype(BF16)) * ecum_e
    lane = lax.broadcasted_iota(jnp.int32, (q, 128), 1)
    ys = []
    for pair in range(HPG // 2):
        xpair = xdt_b[:, pair * 128:(pair + 1) * 128]
        top = jnp.where(lane < SSD_HEADDIM, xpair, jnp.zeros_like(xpair))
        bot = jnp.where(lane >= SSD_HEADDIM, xpair, jnp.zeros_like(xpair))
        atts = []
        for hh in range(2):
            h8 = pair * 2 + hh
            seg = jnp.exp(jnp.where(tril, cum[:, h8:h8 + 1] - cum_t[h8:h8 + 1, :], NEG))
            atts.append((cb * seg).astype(BF16))
        ys.append(_dot(jnp.concatenate(atts, axis=1), jnp.concatenate([top, bot], axis=0)))
    y = y + jnp.concatenate(ys, axis=1) + dskip_ref[...] * xs
    y_ref[...] = y
    h_new = h_prev * elast_e + _dot(bt_ref[...], xw)
    h_scr[...] = h_new

    @pl.when(c == pl.num_programs(1) - 1)
    def _():
        hout_ref[...] = h_new


def _head_expand_matrix(n_heads, rows):
    e = np.zeros((rows, n_heads * SSD_HEADDIM), np.float32)
    for h in range(n_heads):
        e[h, h * SSD_HEADDIM:(h + 1) * SSD_HEADDIM] = 1.0
    return jnp.asarray(e, BF16)


def _ssd_scan_prompt(act, bt, dtr, dt_bias, a_log, d_skip):
    q = SSD_CHUNK
    nc = SEQ // q
    dtr_g = dtr.reshape(SEQ, SSD_GROUPS, HPG).transpose(1, 0, 2)
    dtr_g128 = jnp.pad(dtr_g, ((0, 0), (0, 0), (0, 128 - HPG)))
    dtrt_g = dtr_g.transpose(0, 2, 1)
    pad_r = lambda v: jnp.pad(v.reshape(SSD_GROUPS, 1, HPG), ((0, 0), (0, 0), (0, 128 - HPG)))
    col_c = lambda v: v.reshape(SSD_GROUPS, HPG, 1)
    dskip_e = jnp.repeat(d_skip, SSD_HEADDIM).reshape(1, D_SSD)
    e = _head_expand_matrix(HPG, 128)
    small = lambda shape: pl.BlockSpec((None,) + shape, lambda g, c: (g, 0, 0))
    y, h_out = pl.pallas_call(
        _scan_kernel, grid=(SSD_GROUPS, nc),
        in_specs=[pl.BlockSpec((q, GW), lambda g, c: (c, g)),
                  pl.BlockSpec((q, SSD_STATE), lambda g, c: (c, D_SSD // SSD_STATE + g)),
                  pl.BlockSpec((q, SSD_STATE), lambda g, c: (c, (D_SSD + SSD_GROUPS * SSD_STATE) // SSD_STATE + g)),
                  pl.BlockSpec((SSD_STATE, q), lambda g, c: (g, c)),
                  pl.BlockSpec((None, q, 128), lambda g, c: (g, c, 0)),
                  pl.BlockSpec((None, HPG, q), lambda g, c: (g, 0, c)),
                  small((1, 128)), small((HPG, 1)), small((1, 128)), small((HPG, 1)),
                  pl.BlockSpec((1, GW), lambda g, c: (0, g)),
                  pl.BlockSpec((128, GW), lambda g, c: (0, 0))],
        out_specs=[pl.BlockSpec((q, GW), lambda g, c: (c, g)),
                   pl.BlockSpec((None, SSD_STATE, GW), lambda g, c: (g, 0, 0))],
        out_shape=[jax.ShapeDtypeStruct((SEQ, D_SSD), F32),
                   jax.ShapeDtypeStruct((SSD_GROUPS, SSD_STATE, GW), F32)],
        scratch_shapes=[pltpu.VMEM((SSD_STATE, GW), F32)],
        compiler_params=_cp("parallel", "arbitrary"), name="ssd_scan")(
            act, act, act, bt, dtr_g128, dtrt_g, pad_r(dt_bias), col_c(dt_bias), pad_r(a_log), col_c(a_log),
            dskip_e, e)
    h_last = h_out.reshape(SSD_GROUPS, SSD_STATE, HPG, SSD_HEADDIM).transpose(0, 2, 3, 1)
    return y, h_last.reshape(SSD_HEADS, SSD_HEADDIM, SSD_STATE)


def _conv_sample_kernel(s_ref, x_ref, w_ref, b_ref, o_ref):
    acc = b_ref[...] + x_ref[...] * w_ref[SSD_CONV - 1:SSD_CONV, :]
    for k in range(SSD_CONV - 1):
        acc = acc + s_ref[k] * w_ref[k:k + 1, :]
    o_ref[...] = _silu(acc)


def _conv_sample(state_t, xbc, conv_w, conv_b):
    tc = 1024
    return pl.pallas_call(
        _conv_sample_kernel, grid=(CONV_CH // tc,),
        in_specs=[pl.BlockSpec((SSD_CONV - 1, DEC_BATCH, tc), lambda c: (0, 0, c)),
                  pl.BlockSpec((DEC_BATCH, tc), lambda c: (0, c)),
                  pl.BlockSpec((SSD_CONV, tc), lambda c: (0, c)),
                  pl.BlockSpec((1, tc), lambda c: (0, c))],
        out_specs=pl.BlockSpec((DEC_BATCH, tc), lambda c: (0, c)),
        out_shape=jax.ShapeDtypeStruct((DEC_BATCH, CONV_CH), F32),
        compiler_params=_cp("parallel"), name="conv_sample")(state_t, xbc, conv_w, conv_b.reshape(1, CONV_CH))


def _ssd_step_kernel(h_ref, xs_ref, dtr_ref, bn_ref, cn_ref, dtb_ref, al_ref, dskip_ref, e64_ref, e8_ref,
                     *rest):
    y_ref, hout_ref = rest[-2:]
    n_prev = hout_ref.shape[0] - 1
    if n_prev:
        hout_ref[0:n_prev] = rest[0][...]
    dt = _softplus(dtr_ref[...] + dtb_ref[...])
    dec = jnp.exp(dt * (-jnp.exp(al_ref[...])))
    e64 = e64_ref[...]
    dt_e = _expand(dt, e64)[0:1, :]
    dec_e = _expand(dec, e64)[0:1, :]
    xs = xs_ref[...]
    xdt = xs * dt_e
    e8 = e8_ref[...]
    b_exp = _dot(bn_ref[...].astype(BF16), e8)
    c_exp = _dot(cn_ref[...].astype(BF16), e8)
    h_new = h_ref[...] * dec_e + b_exp * xdt
    hout_ref[n_prev] = h_new
    y_ref[...] = jnp.sum(h_new * c_exp, axis=0, keepdims=True) + dskip_ref[...] * xs


def _ssd_step_sample(state_t, layer, h_acc, act_s, dtr_s, dt_bias, a_log, d_skip):
    b = DEC_BATCH
    xs3 = act_s[:, :D_SSD].reshape(b, 1, D_SSD)
    bn = act_s[:, D_SSD:D_SSD + 1024].reshape(b, SSD_GROUPS, SSD_STATE).transpose(0, 2, 1)
    cn = act_s[:, D_SSD + 1024:].reshape(b, SSD_GROUPS, SSD_STATE).transpose(0, 2, 1)
    bn = jnp.pad(bn, ((0, 0), (0, 0), (0, 128 - SSD_GROUPS)))
    cn = jnp.pad(cn, ((0, 0), (0, 0), (0, 128 - SSD_GROUPS)))
    dtr8 = jnp.pad(dtr_s.reshape(b, 1, SSD_HEADS), ((0, 0), (0, 7), (0, 128 - SSD_HEADS)))
    row128 = lambda v: jnp.pad(v.reshape(1, SSD_HEADS), ((0, 0), (0, 128 - SSD_HEADS)))
    dskip_e = jnp.repeat(d_skip, SSD_HEADDIM).reshape(1, D_SSD)
    e64 = _head_expand_matrix(SSD_HEADS, 128)
    e8np = np.zeros((128, D_SSD), np.float32)
    for g in range(SSD_GROUPS):
        e8np[g, g * GW:(g + 1) * GW] = 1.0
    e8 = jnp.asarray(e8np, BF16)
    full = lambda shape: pl.BlockSpec(shape, lambda i: (0,) * len(shape))
    per_b = lambda shape: pl.BlockSpec((None,) + shape, lambda i: (i, 0, 0))
    per_lb = pl.BlockSpec((None, None, SSD_STATE, D_SSD), lambda i: (layer, i, 0, 0))
    in_specs = [per_lb, per_b((1, D_SSD)), per_b((8, 128)), per_b((SSD_STATE, 128)), per_b((SSD_STATE, 128)),
                full((1, 128)), full((1, 128)), full((1, D_SSD)), full((128, D_SSD)), full((128, D_SSD))]
    args = [state_t, xs3, dtr8, bn, cn, row128(dt_bias), row128(a_log), dskip_e, e64, e8]
    stacked = lambda n: pl.BlockSpec((n, None, SSD_STATE, D_SSD), lambda i: (0, i, 0, 0))
    if h_acc is not None:
        in_specs.append(stacked(layer))
        args.append(h_acc)
    return pl.pallas_call(
        _ssd_step_kernel, grid=(b,), in_specs=in_specs,
        out_specs=[per_b((1, D_SSD)), stacked(layer + 1)],
        out_shape=[jax.ShapeDtypeStruct((b, 1, D_SSD), F32),
                   jax.ShapeDtypeStruct((layer + 1, b, SSD_STATE, D_SSD), F32)],
        compiler_params=_cp("parallel"), name="ssd_step")(*args)


def _chunk_proj_kernel(pt_ref, *refs):
    del pt_ref
    pages = refs[:N_PAGES]
    w_ref, o_ref = refs[N_PAGES:]
    nch = N_PAGES * PAGE_SIZE // CMP_STRIDE
    cpp = PAGE_SIZE // CMP_STRIDE
    for s in range(2):
        lhs_h = []
        for h in range(NSA_KV_HEADS):
            per_r = []
            for r in range(CMP_STRIDE):
                rows = [pg[pl.ds(r, cpp, stride=CMP_STRIDE), h, s, :] for pg in pages]
                per_r.append(jnp.concatenate(rows, axis=0).astype(BF16))
            lhs_h.append(jnp.concatenate(per_r, axis=1))
        res = _dot(jnp.concatenate(lhs_h, axis=0), w_ref[s])
        for h in range(NSA_KV_HEADS):
            o_ref[s, h] = res[h * nch:(h + 1) * nch]


def _chunk_proj(pages_arr, layer, page_table, w1c, n_groups):
    nch = N_PAGES * PAGE_SIZE // CMP_STRIDE

    def page_spec(pg):
        return pl.BlockSpec((None, None, PAGE_SIZE, NSA_KV_HEADS, 2, HEAD_DIM),
                            lambda b, pt: (layer, pt[b * N_PAGES + pg], 0, 0, 0, 0))

    in_specs = [page_spec(p) for p in range(N_PAGES)]
    in_specs.append(pl.BlockSpec((2, CMP_STRIDE * HEAD_DIM, 2 * HEAD_DIM), lambda b, pt: (0, 0, 0)))
    gs = pltpu.PrefetchScalarGridSpec(
        num_scalar_prefetch=1, grid=(n_groups,), in_specs=in_specs,
        out_specs=pl.BlockSpec((2, NSA_KV_HEADS, nch, 2 * HEAD_DIM), lambda b, pt: (0, 0, b, 0)))
    return pl.pallas_call(
        _chunk_proj_kernel, grid_spec=gs,
        out_shape=jax.ShapeDtypeStruct((2, NSA_KV_HEADS, n_groups * nch, 2 * HEAD_DIM), F32),
        compiler_params=_cp("parallel"), name="cmp_chunk_proj")(page_table, *([pages_arr] * N_PAGES), w1c)


def _cmp_bias_kernel(pe_ref, w_ref, b_ref, o_ref):
    o_ref[...] = b_ref[...] + _dot(pe_ref[...], w_ref[...])


def _cmp_bias(cmp_pe, cmp_w1, cmp_b1):
    k = 2 * CMP_STRIDE * HEAD_DIM
    pe = jnp.broadcast_to(cmp_pe.reshape(2, 1, k), (2, 8, k)).astype(BF16)
    w = cmp_w1.reshape(2, k, HEAD_DIM).astype(BF16)
    b = jnp.broadcast_to(cmp_b1.reshape(2, 1, HEAD_DIM), (2, 8, HEAD_DIM))
    return pl.pallas_call(
        _cmp_bias_kernel, grid=(2,),
        in_specs=[pl.BlockSpec((None, 8, k), lambda s: (s, 0, 0)),
                  pl.BlockSpec((None, k, HEAD_DIM), lambda s: (s, 0, 0)),
                  pl.BlockSpec((None, 8, HEAD_DIM), lambda s: (s, 0, 0))],
        out_specs=pl.BlockSpec((None, 8, HEAD_DIM), lambda s: (s, 0, 0)),
        out_shape=jax.ShapeDtypeStruct((2, 8, HEAD_DIM), F32),
        compiler_params=_cp("parallel"), name="cmp_bias")(pe, w, b)


def _cmp_mlp_kernel(p_ref, b_ref, w2_ref, o_ref):
    n = p_ref.shape[2]
    for s in range(2):
        for h in range(NSA_KV_HEADS):
            p = p_ref[s, h]
            nxt = pltpu.roll(p[:, HEAD_DIM:], n - 1, axis=0)
            hid = _silu(p[:, :HEAD_DIM] + nxt + b_ref[s, 0:1, :])
            o_ref[h * 2 + s] = _dot(hid.astype(BF16), w2_ref[s]).astype(o_ref.dtype)


def _cmp_mlp(p, b1c, w2, n_groups, out_dtype):
    nch = p.shape[2] // n_groups
    return pl.pallas_call(
        _cmp_mlp_kernel, grid=(n_groups,),
        in_specs=[pl.BlockSpec((2, NSA_KV_HEADS, nch, 2 * HEAD_DIM), lambda b: (0, 0, b, 0)),
                  pl.BlockSpec((2, 8, HEAD_DIM), lambda b: (0, 0, 0)),
                  pl.BlockSpec((2, HEAD_DIM, HEAD_DIM), lambda b: (0, 0, 0))],
        out_specs=pl.BlockSpec((None, 2 * NSA_KV_HEADS, nch, HEAD_DIM), lambda b: (b, 0, 0, 0)),
        out_shape=jax.ShapeDtypeStruct((n_groups, 2 * NSA_KV_HEADS, nch, HEAD_DIM), out_dtype),
        compiler_params=_cp("parallel"), name="cmp_mlp")(p, b1c, w2)


def _bucket_np(dist):
    n = np.maximum(dist, 0)
    exact = REL_BUCKETS // 2
    nf = np.maximum(n, 1).astype(np.float32)
    big = exact + (np.log(nf / np.float32(exact)) / np.float32(math.log(REL_MAX_DIST / exact))
                   * np.float32(REL_BUCKETS - exact)).astype(np.int32)
    return np.where(n < exact, n, np.minimum(big, REL_BUCKETS - 1))


def _bias_table(rel_bias_t, dist, valid):
    tab = rel_bias_t[:, _bucket_np(dist)]
    return jnp.where(jnp.asarray(valid), tab, NEG)


CMP_PAD = 120
CMP_ROWS = 768


def _wsel_np(n_cmp, n_blocks, rows, pad):
    w = np.zeros((rows, 128), np.float32)
    c0 = np.arange(n_cmp)[:, None] * CMP_STRIDE
    s0 = np.arange(n_blocks)[None, :] * SEL_BLOCK
    inter = np.minimum(c0 + 2 * CMP_STRIDE, s0 + SEL_BLOCK) - np.maximum(c0, s0)
    w[pad:pad + n_cmp, :n_blocks] = np.maximum(inter, 0).astype(np.float32) / (2 * CMP_STRIDE)
    return w


M0 = -1e20


def _nsa_prompt_kernel(q_ref, kcp_ref, vcp_ref, wselw_ref, kco_ref, vco_ref, wselo_ref,
                       ks_ref, vs_ref, kw_ref, vw_ref, small_ref, tc_ref, td_ref, tp_ref, cb_ref,
                       o_ref, m_scr, l_scr, acc_scr, imp_scr, out_scr, sel_scr, gate_scr):
    g = pl.program_id(0)
    i = pl.program_id(1)
    qb = Q_BLOCK
    w = Q_PER_KV * qb
    qf = q_ref[...]
    qt = jnp.concatenate([qf[:, j * HEAD_DIM:(j + 1) * HEAD_DIM].T.astype(BF16) for j in range(Q_PER_KV)],
                         axis=1)
    row = lax.broadcasted_iota(jnp.int32, (qb, qb), 0)
    col = lax.broadcasted_iota(jnp.int32, (qb, qb), 1)
    row4 = lax.broadcasted_iota(jnp.int32, (qb, w), 0)
    cb = cb_ref[...]
    tile4 = lambda x: jnp.concatenate([x] * Q_PER_KV, axis=1)

    gate_scr[...] = jax.nn.sigmoid(small_ref[...]).T

    def gate_row(branch):
        start = GATE_LANE0 + branch * NSA_HEADS + g * Q_PER_KV
        r8 = gate_scr[pl.ds(pl.multiple_of((start // 8) * 8, 8), 8), :]
        r4 = jnp.where(start % 8 == 0, r8[0:Q_PER_KV], r8[Q_PER_KV:2 * Q_PER_KV])
        return jnp.concatenate([r4[j:j + 1, :] for j in range(Q_PER_KV)], axis=1)

    def reset():
        m_scr[...] = jnp.full(m_scr.shape, M0, F32)
        l_scr[...] = jnp.zeros(l_scr.shape, F32)
        acc_scr[...] = jnp.zeros(acc_scr.shape, F32)

    def flash(parts, wsel_t=None):
        ss = []
        for k, _, bias, mask in parts:
            s = _dot(k, qt) * (SCALE * LOG2E) + bias
            ss.append(s if mask is None else jnp.where(mask, s, NEG))
        m_prev = m_scr[...]
        m_new = m_prev
        for s in ss:
            m_new = jnp.maximum(m_new, jnp.max(s, axis=0, keepdims=True))
        alpha = jnp.exp2(m_prev - m_new)
        ps = [jnp.exp2(s - m_new) for s in ss]
        lsum = jnp.sum(ps[0], axis=0, keepdims=True)
        for p in ps[1:]:
            lsum = lsum + jnp.sum(p, axis=0, keepdims=True)
        l_scr[...] = alpha * l_scr[...] + lsum
        m_scr[...] = m_new
        pb = jnp.concatenate([p.astype(BF16) for p in ps], axis=0)
        vt = jnp.concatenate([part[1] for part in parts], axis=1)
        acc_scr[...] = acc_scr[...] * alpha + _dot(vt, pb)
        if wsel_t is not None:
            imp_scr[...] = imp_scr[...] * alpha + _dot(wsel_t, pb)

    def normalized(ref):
        return ref[...] / jnp.maximum(l_scr[...], 1e-30)

    def lane_cat(ref, n):
        return jnp.concatenate([ref[u] for u in range(n)], axis=1)

    reset()
    imp_scr[...] = jnp.zeros(imp_scr.shape, F32)
    per_q = qb // CMP_STRIDE
    npos = lax.broadcasted_iota(jnp.int32, (4 * qb, w), 0)
    d0 = pl.multiple_of(i * per_q, 8)
    kcw = kcp_ref[pl.ds(d0, qb), :].astype(BF16)
    vcw = vcp_ref[pl.ds(d0, qb), :].T.astype(BF16)
    flash([(kco_ref[...].reshape(4 * qb, HEAD_DIM), lane_cat(vco_ref, 4), cb,
            (npos >= CMP_PAD) & (npos < i * per_q)),
           (kcw, vcw, tc_ref[...], (row4 + i * per_q) >= CMP_PAD)],
          jnp.concatenate([lane_cat(wselo_ref, 4), wselw_ref[...]], axis=1))
    out_scr[...] = gate_row(0) * normalized(acc_scr)
    impn = normalized(imp_scr)
    imp = impn[:, 0:qb]
    for j in range(1, Q_PER_KV):
        imp = imp + impn[:, j * qb:(j + 1) * qb]

    rowf = row.astype(F32)
    cur = 2 * i + (col >= SEL_BLOCK).astype(jnp.int32)
    forced = (row == 0) | (row == cur) | (row == cur - 1)
    score = jnp.where(forced, FORCED_SCORE, jnp.where(row <= cur, imp, -1.0))
    sel = jnp.zeros((qb, qb), F32)
    for _ in range(N_SEL):
        mx = jnp.max(score, axis=0, keepdims=True)
        idx = jnp.min(jnp.where(score == mx, rowf, 1e9), axis=0, keepdims=True)
        hit = rowf == idx
        sel = jnp.where(hit, 1.0, sel)
        score = jnp.where(hit, -3e38, score)
    sel_scr[...] = sel

    def block_rows(r, n_blocks):
        rows = [jnp.broadcast_to(r[u:u + 1, :], (SEL_BLOCK, qb)) for u in range(n_blocks)]
        return tile4(jnp.concatenate(rows, axis=0)) > 0.5

    reset()
    gt = 4
    n_full = jnp.maximum(i - 1, 0) // gt

    def sel_body(gi, carry):
        t0 = gi * gt
        r8 = sel_scr[pl.ds(pl.multiple_of(gi * 2 * gt, 8), 2 * gt), :]
        k = ks_ref[pl.ds(t0, gt)].reshape(gt * qb, HEAD_DIM)
        vt = jnp.concatenate([vs_ref[t0 + u] for u in range(gt)], axis=1)
        flash([(k, vt, cb, block_rows(r8, 2 * gt))])
        return carry

    lax.fori_loop(0, n_full, sel_body, 0)

    t0 = n_full * gt
    last = pl.num_programs(1) - 1
    r16 = jnp.concatenate(
        [sel_scr[pl.ds(pl.multiple_of(t0 * 2, 8), 8), :],
         sel_scr[pl.ds(pl.multiple_of(jnp.minimum(t0 * 2 + 8, qb - 8), 8), 8), :]], axis=0)
    tdv = td_ref[...]
    tpv = tp_ref[...]
    parts = []
    for u in range(gt + 1):
        kt = t0 + u
        ktc = jnp.minimum(kt, last)
        bias = jnp.where(kt == i, tdv, jnp.where(kt == i - 1, tpv, cb))
        mask = block_rows(r16[2 * u:2 * u + 2], 2) & (kt <= i)
        parts.append((ks_ref[ktc], vs_ref[ktc], bias, mask))
    flash(parts)
    out_scr[...] = out_scr[...] + gate_row(1) * normalized(acc_scr)

    reset()
    upper = tile4((row > col).astype(F32)) > 0.5
    n_back = WINDOW // qb
    parts = []
    for u in range(n_back + 1):
        kt = i - n_back + u
        ktc = jnp.maximum(kt, 0)
        bias = tdv if u == n_back else (tpv if u == n_back - 1 else cb)
        mask = (upper & (kt >= 0)) if u == 0 else jnp.broadcast_to(kt >= 0, (qb, w))
        parts.append((kw_ref[ktc], vw_ref[ktc], bias, mask))
    flash(parts)
    out = out_scr[...] + gate_row(2) * normalized(acc_scr)
    for j in range(Q_PER_KV):
        o_ref[:, j * HEAD_DIM:(j + 1) * HEAD_DIM] = out[:, j * qb:(j + 1) * qb].T.astype(o_ref.dtype)


def _nsa_prompt(proj, kcv, kv_sel, kv_win, rel_bias, seq):
    qb = Q_BLOCK
    nb = seq // qb
    hk = NSA_KV_HEADS
    w = Q_PER_KV * qb

    def key_tiles(x):
        x5 = x.astype(BF16).reshape(nb, qb, hk, 2, HEAD_DIM)
        return x5[:, :, :, 0, :].transpose(2, 0, 1, 3), x5[:, :, :, 1, :].transpose(2, 0, 3, 1)

    ks, vs = key_tiles(kv_sel)
    kw, vw = key_tiles(kv_win)

    nck = kcv.shape[1]
    padded = jnp.pad(kcv, ((0, 0), (CMP_PAD, CMP_ROWS - CMP_PAD - nck), (0, 0)))
    kc_pad, vc_pad = padded[0::2], padded[1::2]
    n_old = 4 * qb
    kco = kc_pad[:, :n_old].astype(BF16).reshape(hk, 4, qb, HEAD_DIM)
    vco = vc_pad[:, :n_old].astype(BF16).reshape(hk, 4, qb, HEAD_DIM).transpose(0, 1, 3, 2)
    widx = (qb // CMP_STRIDE) * np.arange(nb)[:, None] + np.arange(qb)[None, :]
    wsel = _wsel_np(seq // CMP_STRIDE - 1, seq // SEL_BLOCK, CMP_ROWS, CMP_PAD)
    wselo = jnp.asarray(wsel[:n_old].reshape(4, qb, 128).transpose(0, 2, 1), BF16)
    wselw = jnp.asarray(wsel[widx].transpose(0, 2, 1), BF16)

    rbt = rel_bias.T
    a = np.arange(qb)[:, None]
    b = np.arange(qb)[None, :]
    lanes = lambda t16: (t16 * LOG2E).reshape(hk, Q_PER_KV, qb, qb).transpose(0, 3, 1, 2).reshape(hk, qb, w)
    td = lanes(_bias_table(rbt, a - b, a >= b))
    tp = lanes(_bias_table(rbt, a - b + qb, np.ones((qb, qb), bool)))
    dist_c = a - CMP_STRIDE * b + (CMP_STRIDE * CMP_PAD - 2 * CMP_STRIDE + 1)
    tc = lanes(_bias_table(rbt, dist_c, dist_c >= 0))
    cb = jnp.repeat(rel_bias[REL_BUCKETS - 1].reshape(hk, Q_PER_KV) * LOG2E, qb, axis=1).reshape(hk, 1, w)

    per_g = lambda shape: pl.BlockSpec((None,) + shape, lambda g, i: (g,) + (0,) * len(shape))
    return pl.pallas_call(
        _nsa_prompt_kernel, grid=(hk, nb),
        in_specs=[pl.BlockSpec((qb, w), lambda g, i: (i, C_Q // w + g)),
                  per_g((CMP_ROWS, HEAD_DIM)), per_g((CMP_ROWS, HEAD_DIM)),
                  pl.BlockSpec((None, qb, qb), lambda g, i: (i, 0, 0)),
                  per_g((4, qb, qb)), per_g((4, qb, qb)), pl.BlockSpec((4, qb, qb), lambda g, i: (0, 0, 0)),
                  per_g((nb, qb, qb)), per_g((nb, qb, qb)), per_g((nb, qb, qb)), per_g((nb, qb, qb)),
                  pl.BlockSpec((qb, 256), lambda g, i: (i, C_SMALL // 256)),
                  per_g((qb, w)), per_g((qb, w)), per_g((qb, w)), per_g((1, w))],
        out_specs=pl.BlockSpec((qb, w), lambda g, i: (i, g)),
        out_shape=jax.ShapeDtypeStruct((seq, NSA_HEADS * HEAD_DIM), BF16),
        scratch_shapes=[pltpu.VMEM((1, w), F32), pltpu.VMEM((1, w), F32), pltpu.VMEM((HEAD_DIM, w), F32),
                        pltpu.VMEM((qb, w), F32), pltpu.VMEM((HEAD_DIM, w), F32), pltpu.VMEM((qb, qb), F32),
                        pltpu.VMEM((256, qb), F32)],
        compiler_params=_cp("parallel", "arbitrary"), name="nsa_prompt")(
            proj, kc_pad, vc_pad, wselw, kco, vco, wselo, ks, vs, kw, vw, proj, tc, td, tp, cb)


N_BLK_S = -(-(PAST_LEN + 1) // SEL_BLOCK)
N_OLD_SEL = N_SEL - 1


def _nsa_sample_cmp_kernel(q_ref, kcv_ref, bias_ref, wsel_ref, oc_ref, idx_ref):
    hrow = lax.broadcasted_iota(jnp.int32, (NSA_HEADS, 128), 0) // Q_PER_KV
    psums = []
    bias = bias_ref[...]
    valid = bias > 0.5 * NEG
    for bb in range(q_ref.shape[0]):
        q = q_ref[bb].astype(BF16)
        o_c = jnp.zeros((NSA_HEADS, HEAD_DIM), F32)
        for g in range(NSA_KV_HEADS):
            s = jnp.where(valid, _dot_nt(q, kcv_ref[bb, 2 * g]) * SCALE + bias, NEG)
            mx = jnp.max(s, axis=-1, keepdims=True)
            e = jnp.where(valid, jnp.exp(s - mx), 0.0)
            p = e / jnp.maximum(jnp.sum(e, axis=-1, keepdims=True), 1e-30)
            mine = hrow == g
            o_c = o_c + jnp.where(mine, _dot(p.astype(BF16), kcv_ref[bb, 2 * g + 1]), 0.0)
            psums.append(jnp.sum(jnp.where(mine, p, 0.0), axis=0, keepdims=True))
        oc_ref[bb] = o_c
    psum = jnp.concatenate(psums, axis=0)
    imp = _dot(psum.astype(BF16), wsel_ref[...])
    lane = lax.broadcasted_iota(jnp.int32, imp.shape, 1)
    cur = N_BLK_S - 1
    forced = (lane == 0) | (lane == cur - 1)
    score = jnp.where(lane >= cur, -3e38, jnp.where(forced, FORCED_SCORE, imp))
    lanef = lane.astype(F32)
    out = jnp.zeros(imp.shape, F32)
    for r in range(N_OLD_SEL):
        mx = jnp.max(score, axis=-1, keepdims=True)
        idx = jnp.min(jnp.where(score == mx, lanef, 1e9), axis=-1, keepdims=True)
        out = jnp.where(lane == r, idx, out)
        score = jnp.where(lanef == idx, -3e38, score)
    idx_ref[...] = out.astype(jnp.int32)


def _nsa_sample_cmp(q_s, kcv_s, rel_bias):
    b = DEC_BATCH
    rbt = rel_bias.T
    n = np.arange(128)[None, :]
    dist = PAST_LEN - (CMP_STRIDE * n + 2 * CMP_STRIDE - 1)
    bias = _bias_table(rbt, dist, n < (PAST_LEN // CMP_STRIDE - 1))[:, 0, :]
    wsel = jnp.asarray(_wsel_np(PAST_LEN // CMP_STRIDE - 1, N_BLK_S, 128, 0), BF16)
    bb = 8
    return pl.pallas_call(
        _nsa_sample_cmp_kernel, grid=(b // bb,),
        in_specs=[pl.BlockSpec((bb, NSA_HEADS, HEAD_DIM), lambda i: (i, 0, 0)),
                  pl.BlockSpec((bb, 2 * NSA_KV_HEADS, 128, HEAD_DIM), lambda i: (i, 0, 0, 0)),
                  pl.BlockSpec((NSA_HEADS, 128), lambda i: (0, 0)),
                  pl.BlockSpec((128, 128), lambda i: (0, 0))],
        out_specs=[pl.BlockSpec((bb, NSA_HEADS, HEAD_DIM), lambda i: (i, 0, 0)),
                   pl.BlockSpec((bb * NSA_KV_HEADS, 128), lambda i: (i, 0))],
        out_shape=[jax.ShapeDtypeStruct((b, NSA_HEADS, HEAD_DIM), F32),
                   jax.ShapeDtypeStruct((b * NSA_KV_HEADS, 128), jnp.int32)],
        compiler_params=_cp("parallel"), name="nsa_sample_cmp")(q_s, kcv_s, bias, wsel)


def _nsa_sample_attn_kernel(pt_ref, idx_ref, *refs):
    del pt_ref
    blocks = refs[:N_OLD_SEL]
    (win_ref, news_ref, neww_ref, q_ref, oc_ref, gn_ref, bs_ref, bw_ref, b0_ref, o_ref) = refs[N_OLD_SEL:]
    b = pl.program_id(0)
    g = pl.program_id(1)
    q = q_ref[...].astype(BF16)
    gates = jax.nn.sigmoid(gn_ref[...])
    b0 = b0_ref[...]

    def attend(pieces, new_ref):
        new = new_ref[...]
        s_new = jnp.sum(q.astype(F32) * new[:, :HEAD_DIM].astype(BF16).astype(F32), axis=-1, keepdims=True)
        s_new = s_new * SCALE + b0
        mx = s_new
        for s, valid, _ in pieces:
            mx = jnp.maximum(mx, jnp.max(s, axis=-1, keepdims=True))
        e_new = jnp.exp(s_new - mx)
        den = e_new
        acc = e_new.astype(BF16).astype(F32) * new[:, HEAD_DIM:].astype(BF16).astype(F32)
        for s, valid, v in pieces:
            e = jnp.exp(s - mx)
            if valid is not None:
                e = jnp.where(valid, e, 0.0)
            den = den + jnp.sum(e, axis=-1, keepdims=True)
            acc = acc + _dot(e.astype(BF16), v)
        return acc / jnp.maximum(den, 1e-30)

    def interleaved(ref, rows):
        kv = ref[...].reshape(2 * rows, HEAD_DIM)
        return kv.astype(BF16), pltpu.roll(kv, 2 * rows - 1, axis=0).astype(BF16)

    pieces = []
    for kk in range(N_OLD_SEL):
        kvb, vsb = interleaved(blocks[kk], SEL_BLOCK)
        m = idx_ref[(b * NSA_KV_HEADS + g) * N_OLD_SEL + kk]
        s = _dot_nt(q, kvb) * SCALE + bs_ref[m]
        pieces.append((s, None, vsb))
    o_s = attend(pieces, news_ref)

    kvb, vsb = interleaved(win_ref, WINDOW)
    bw = bw_ref[...]
    valid_w = bw > 0.5 * NEG
    s_w = jnp.where(valid_w, _dot_nt(q, kvb) * SCALE + bw, NEG)
    o_w = attend([(s_w, valid_w, vsb)], neww_ref)

    out = gates[:, 0:1] * oc_ref[...] + gates[:, 1:2] * o_s + gates[:, 2:3] * o_w
    o_ref[...] = out.astype(o_ref.dtype)


def _nsa_sample_attn(layer, cache_sel, cache_win, page_table, idx, kvs_new, kvw_new, q_s, o_c, gn_s, rel_bias):
    b = DEC_BATCH
    hk = NSA_KV_HEADS
    rbt = rel_bias.T
    m = np.arange(N_BLK_S - 1)[:, None]
    r = np.arange(SEL_BLOCK)[None, :]
    bs = _bias_table(rbt, PAST_LEN - (SEL_BLOCK * m + r), np.ones((N_BLK_S - 1, SEL_BLOCK), bool))
    bs = bs.reshape(hk, Q_PER_KV, N_BLK_S - 1, SEL_BLOCK).transpose(0, 2, 1, 3)
    odd_masked = lambda t: jnp.stack([t, jnp.full(t.shape, NEG, F32)], axis=-1).reshape(t.shape[:-1] + (-1,))
    bs = odd_masked(bs)
    rw = np.arange(WINDOW)[None, :]
    bw = odd_masked(_bias_table(rbt, WINDOW - rw, rw >= 1)[:, 0, :].reshape(hk, Q_PER_KV, WINDOW))
    b0 = rel_bias[0].reshape(hk, Q_PER_KV, 1)
    per_page = PAGE_SIZE // SEL_BLOCK
    csel = cache_sel.reshape(DEPTH, -1, per_page, SEL_BLOCK, hk, 2, HEAD_DIM)

    def sel_spec(kk):
        def imap(bi, g, pt, ix):
            blk = ix[(bi * hk + g) * N_OLD_SEL + kk]
            return (layer, pt[bi * N_PAGES + blk // per_page], blk % per_page, 0, g, 0, 0)
        return pl.BlockSpec((None, None, None, SEL_BLOCK, None, 2, HEAD_DIM), imap)

    bgj = lambda last: pl.BlockSpec((None, None, Q_PER_KV, last), lambda bi, g, pt, ix: (bi, g, 0, 0))
    new_spec = pl.BlockSpec((None, None, 1, 2 * HEAD_DIM), lambda bi, g, pt, ix: (bi, g, 0, 0))
    in_specs = [sel_spec(kk) for kk in range(N_OLD_SEL)] + [
        pl.BlockSpec((None, None, WINDOW, None, 2, HEAD_DIM), lambda bi, g, pt, ix: (layer, bi, 0, g, 0, 0)),
        new_spec, new_spec, bgj(HEAD_DIM), bgj(HEAD_DIM), bgj(3),
        pl.BlockSpec((None, N_BLK_S - 1, Q_PER_KV, 2 * SEL_BLOCK), lambda bi, g, pt, ix: (g, 0, 0, 0)),
        pl.BlockSpec((None, Q_PER_KV, 2 * WINDOW), lambda bi, g, pt, ix: (g, 0, 0)),
        pl.BlockSpec((None, Q_PER_KV, 1), lambda bi, g, pt, ix: (g, 0, 0))]
    gs = pltpu.PrefetchScalarGridSpec(
        num_scalar_prefetch=2, grid=(b, hk), in_specs=in_specs,
        out_specs=bgj(HEAD_DIM))
    return pl.pallas_call(
        _nsa_sample_attn_kernel, grid_spec=gs,
        out_shape=jax.ShapeDtypeStruct((b, hk, Q_PER_KV, HEAD_DIM), F32),
        compiler_params=_cp("parallel", "parallel"), name="nsa_sample_attn")(
            page_table.reshape(-1), idx.reshape(-1), *([csel] * N_OLD_SEL), cache_win,
            kvs_new.reshape(b, hk, 1, 2 * HEAD_DIM), kvw_new.reshape(b, hk, 1, 2 * HEAD_DIM),
            q_s.reshape(b, hk, Q_PER_KV, HEAD_DIM), o_c.reshape(b, hk, Q_PER_KV, HEAD_DIM), gn_s, bs, bw, b0)


def _router_kernel(h_ref, w_ref, b_ref, o_ref):
    logits = _dot(h_ref[...], w_ref[...]) + b_ref[...]
    lane = lax.broadcasted_iota(jnp.int32, logits.shape, 1)
    lanef = lane.astype(F32)

    def softmax(mask):
        s = jnp.where(mask, logits, NEG)
        e = jnp.where(mask, jnp.exp(s - jnp.max(s, axis=-1, keepdims=True)), 0.0)
        return e / jnp.sum(e, axis=-1, keepdims=True)

    def top1(p, mask):
        mx = jnp.max(jnp.where(mask, p, -1.0), axis=-1, keepdims=True)
        idx = jnp.min(jnp.where(mask & (p == mx), lanef, 1e9), axis=-1, keepdims=True)
        return mx, idx

    gmask = (lane >= N_EXPERTS) & (lane < N_EXPERTS + MOE_GROUPS)
    pg = softmax(gmask)
    p_top, g_idx = top1(pg, gmask)
    e_lo = (g_idx - N_EXPERTS) * MOE_PER_GROUP
    emask = (lanef >= e_lo) & (lanef < e_lo + MOE_PER_GROUP)
    pe = softmax(emask)
    w_a, i_a = top1(pe, emask)
    mask_b = emask & (lanef != i_a)
    w_b, i_b = top1(pe, mask_b)
    tot = w_a + w_b
    gate = jnp.where(lanef == i_a, w_a / tot * p_top, jnp.where(lanef == i_b, w_b / tot * p_top, 0.0))
    o_ref[...] = gate


def _router(hn, w_group, b_group, w_exp, b_exp):
    w = jnp.concatenate([w_exp, w_group, jnp.zeros((D_MODEL, 128 - N_EXPERTS - MOE_GROUPS), F32)], axis=1)
    bias = jnp.concatenate([b_exp, b_group, jnp.zeros((128 - N_EXPERTS - MOE_GROUPS,), F32)]).reshape(1, 128)
    return pl.pallas_call(
        _router_kernel, grid=(M_ALL // TM,),
        in_specs=[pl.BlockSpec((TM, D_MODEL), lambda i: (i, 0)),
                  pl.BlockSpec((D_MODEL, 128), lambda i: (0, 0)),
                  pl.BlockSpec((1, 128), lambda i: (0, 0))],
        out_specs=pl.BlockSpec((TM, 128), lambda i: (i, 0)),
        out_shape=jax.ShapeDtypeStruct((M_ALL, 128), F32),
        compiler_params=_cp("parallel"), name="moe_router")(hn, w.astype(BF16), bias)


def _moe_kernel(h_ref, win_ref, wout_ref, gate_ref, x_ref, o_ref, acc_scr):
    e = pl.program_id(1)

    @pl.when(e == 0)
    def _():
        acc_scr[...] = jnp.zeros(acc_scr.shape, F32)

    u = _dot(h_ref[...], win_ref[...])
    gate = gate_ref[...]
    lane = lax.broadcasted_iota(jnp.int32, gate.shape, 1)
    ge = jnp.sum(jnp.where(lane == e, gate, 0.0), axis=-1, keepdims=True)
    act = _silu(u[:, :D_EXPERT]) * u[:, D_EXPERT:] * ge
    acc_scr[...] = acc_scr[...] + _dot(act.astype(BF16), wout_ref[...])

    @pl.when(e == N_EXPERTS - 1)
    def _():
        o_ref[...] = x_ref[...] + acc_scr[...]


def _moe(hn, gate, x, w_in_e, w_out_e):
    return pl.pallas_call(
        _moe_kernel, grid=(M_ALL // TM, N_EXPERTS),
        in_specs=[pl.BlockSpec((TM, D_MODEL), lambda i, e: (i, 0)),
                  pl.BlockSpec((None, D_MODEL, 2 * D_EXPERT), lambda i, e: (e, 0, 0)),
                  pl.BlockSpec((None, D_EXPERT, D_MODEL), lambda i, e: (e, 0, 0)),
                  pl.BlockSpec((TM, 128), lambda i, e: (i, 0)),
                  pl.BlockSpec((TM, D_MODEL), lambda i, e: (i, 0))],
        out_specs=pl.BlockSpec((TM, D_MODEL), lambda i, e: (i, 0)),
        out_shape=jax.ShapeDtypeStruct((M_ALL, D_MODEL), F32),
        scratch_shapes=[pltpu.VMEM((TM, D_MODEL), F32)],
        compiler_params=_cp("parallel", "arbitrary"), name="moe_experts")(hn, w_in_e, w_out_e, gate, x)


def _prep_w_in(w):
    offs = np.cumsum((0,) + IN_SIZES)
    z, xbc, dt, q, kvc, kvs, kvw, gn, gm = [w[:, offs[k]:offs[k + 1]] for k in range(len(IN_SIZES))]
    small = jnp.concatenate([dt, gn, jnp.zeros((D_MODEL, 256 - dt.shape[1] - gn.shape[1]), w.dtype)], axis=1)
    return jnp.concatenate([z, xbc, q, kvc, kvs, kvw, gm, small], axis=1).astype(BF16)


def _win_out_kernel(cache_ref, new_ref, o_ref):
    sub = NSA_KV_HEADS * 2
    old = cache_ref[...].reshape(WINDOW * sub, HEAD_DIM)
    new = new_ref[...].reshape(sub, HEAD_DIM)
    o_ref[...] = jnp.concatenate([old[sub:], new], axis=0).reshape(o_ref.shape)


def _win_out(cache_win_kv, new_rows):
    blk = lambda rows: pl.BlockSpec((None, None, rows, NSA_KV_HEADS, 2, HEAD_DIM), lambda l, b: (l, b, 0, 0, 0, 0))
    return pl.pallas_call(
        _win_out_kernel, grid=(DEPTH, DEC_BATCH), in_specs=[blk(WINDOW), blk(1)], out_specs=blk(WINDOW),
        out_shape=jax.ShapeDtypeStruct(cache_win_kv.shape, cache_win_kv.dtype),
        compiler_params=_cp("parallel", "parallel"), name="win_out")(cache_win_kv, new_rows)


def _layer(l, x, p, cache_cmp_kv, cache_sel_kv, cache_win_kv, state_t, h_acc, state_conv_l, page_table, rel_bias):
    s = SEQ
    b = DEC_BATCH
    h1 = _rmsnorm(x, p['norm1_w'], BF16)
    proj = _mm(h1, _prep_w_in(p['w_in']), tn=1792, out_dtype=F32, name="in_proj")

    kv_new = {name: proj[:, c:c + KV_COLS] for name, c in (('cmp', C_KVC), ('sel', C_KVS), ('win', C_KVW))}

    act_p = _conv_prompt(proj, p['conv_w'], p['conv_b'])
    bt = act_p[:, D_SSD:D_SSD + SSD_GROUPS * SSD_STATE].T.astype(BF16)
    dtr = proj[:, C_SMALL:C_SMALL + SSD_HEADS]
    y_p, ssm_p = _ssd_scan_prompt(act_p, bt, dtr[:s], p['dt_bias'], p['a_log'], p['d_skip'])
    conv_p = proj[s - (SSD_CONV - 1):s, C_XBC:C_XBC + CONV_CH]

    xbc_s = proj[s:, C_XBC:C_XBC + CONV_CH]
    act_s = _conv_sample(state_conv_l.transpose(1, 0, 2), xbc_s, p['conv_w'], p['conv_b'])
    y_s, h_acc = _ssd_step_sample(state_t, l, h_acc, act_s, dtr[s:], p['dt_bias'], p['a_log'], p['d_skip'])
    conv_s = jnp.concatenate([state_conv_l[:, 1:], xbc_s[:, None, :]], axis=1)

    yn = _gated_norm(y_p, y_s.reshape(b, D_SSD), proj, p['ssd_norm_w'])
    y_a = _mm(yn, p['w_ssd_out'].astype(BF16), tn=1024, out_dtype=F32, name="ssd_out")

    b1c = _cmp_bias(p['cmp_pe'], p['cmp_w1'], p['cmp_b1'])
    w1c = (p['cmp_w1'].reshape(2, 2, CMP_STRIDE * HEAD_DIM, HEAD_DIM).transpose(0, 2, 1, 3)
           .reshape(2, CMP_STRIDE * HEAD_DIM, 2 * HEAD_DIM).astype(BF16))
    w2 = p['cmp_w2'].astype(BF16)

    n_grp_p = s // (N_PAGES * PAGE_SIZE)
    kvc_pages = kv_new['cmp'][:s].reshape(1, s // PAGE_SIZE, PAGE_SIZE, NSA_KV_HEADS, 2, HEAD_DIM)
    pt_p = jnp.arange(n_grp_p * N_PAGES, dtype=jnp.int32)
    pp = _chunk_proj(kvc_pages, 0, pt_p, w1c, n_grp_p)
    kcv_p = _cmp_mlp(pp, b1c, w2, 1, F32)[0]
    attn_p = _nsa_prompt(proj, kcv_p, kv_new['sel'][:s], kv_new['win'][:s], rel_bias, s)

    ps = _chunk_proj(cache_cmp_kv, l, page_table.reshape(-1), w1c, b)
    kcv_s = _cmp_mlp(ps, b1c, w2, b, BF16)
    q_s = proj[s:, C_Q:C_Q + NSA_HEADS * HEAD_DIM].reshape(b, NSA_HEADS, HEAD_DIM)
    o_c, idx = _nsa_sample_cmp(q_s, kcv_s, rel_bias)
    gn_s = (proj[s:, C_SMALL + GATE_LANE0:C_SMALL + GATE_LANE0 + 3 * NSA_HEADS]
            .reshape(b, 3, NSA_KV_HEADS, Q_PER_KV).transpose(0, 2, 3, 1))
    attn_s = _nsa_sample_attn(l, cache_sel_kv, cache_win_kv, page_table, idx[:, :N_OLD_SEL],
                              kv_new['sel'][s:], kv_new['win'][s:], q_s, o_c, gn_s, rel_bias)
    attn = jnp.concatenate([attn_p, attn_s.reshape(b, NSA_HEADS * HEAD_DIM).astype(BF16)], axis=0)

    tn = 1024
    m = _mm(attn, p['w_nsa_out'].astype(BF16), tn=tn, out_dtype=BF16, name="nsa_out_merge",
            extras=((y_a, 0), (proj, C_GM // tn), (proj, (C_GM + D_MODEL) // tn)), epilogue=_merge_epilogue)
    x = _mm(m, p['w_out'].astype(BF16), tn=tn, out_dtype=F32, name="out_proj", extras=((x, 0),),
            epilogue=_resid_epilogue)
    h2 = _rmsnorm(x, p['norm2_w'], BF16)
    gate = _router(h2, p['w_router_group'], p['b_router_group'], p['w_router_exp'], p['b_router_exp'])
    x = _moe(h2, gate, x, p['w_exp_in'].astype(BF16), p['w_exp_out'].astype(BF16))

    kv5 = lambda a: a.reshape(a.shape[0], NSA_KV_HEADS, 2, HEAD_DIM)
    outs = dict(
        cmp_p=kv5(kv_new['cmp'][:s])[None], cmp_s=kv5(kv_new['cmp'][s:])[:, None],
        sel_p=kv5(kv_new['sel'][:s])[None], sel_s=kv5(kv_new['sel'][s:])[:, None],
        win_p=kv5(kv_new['win'][s - WINDOW:s])[None], win_new=kv5(kv_new['win'][s:])[:, None],
        ssm_p=ssm_p[None], conv_p=conv_p[None], conv_s=conv_s)
    return x, h_acc, outs


def kernel(x_prompt, x_sample, cache_cmp_kv, cache_sel_kv, cache_win_kv, state_ssm, state_conv, page_table,
           rel_bias, norm1_w, norm2_w, final_norm_w, w_in, conv_w, conv_b, dt_bias, a_log, d_skip, ssd_norm_w,
           w_ssd_out, cmp_pe, cmp_w1, cmp_b1, cmp_w2, w_nsa_out, w_out, w_router_group, b_router_group,
           w_router_exp, b_router_exp, w_exp_in, w_exp_out):
    x = jnp.concatenate([x_prompt[0], x_sample[:, 0]], axis=0)
    state_t = state_ssm.reshape(DEPTH, DEC_BATCH, D_SSD, SSD_STATE).transpose(0, 1, 3, 2)
    h_acc = None
    per_layer = []
    for l in range(DEPTH):
        p = {'norm1_w': norm1_w[l], 'norm2_w': norm2_w[l], 'w_in': w_in[l], 'conv_w': conv_w[l],
             'conv_b': conv_b[l], 'dt_bias': dt_bias[l], 'a_log': a_log[l], 'd_skip': d_skip[l],
             'ssd_norm_w': ssd_norm_w[l], 'w_ssd_out': w_ssd_out[l], 'cmp_pe': cmp_pe[l], 'cmp_w1': cmp_w1[l],
             'cmp_b1': cmp_b1[l], 'cmp_w2': cmp_w2[l], 'w_nsa_out': w_nsa_out[l], 'w_out': w_out[l],
             'w_router_group': w_router_group[l], 'b_router_group': b_router_group[l],
             'w_router_exp': w_router_exp[l], 'b_router_exp': b_router_exp[l],
             'w_exp_in': w_exp_in[l], 'w_exp_out': w_exp_out[l]}
        x, h_acc, outs = _layer(l, x, p, cache_cmp_kv, cache_sel_kv, cache_win_kv, state_t, h_acc, state_conv[l],
                                page_table, rel_bias)
        per_layer.append(outs)
    y = _rmsnorm(x, final_norm_w, F32)
    stack = lambda k: jnp.stack([o[k] for o in per_layer])
    ssm_s = h_acc.transpose(0, 1, 3, 2).reshape(DEPTH, DEC_BATCH, SSD_HEADS, SSD_HEADDIM, SSD_STATE)
    win_s = _win_out(cache_win_kv, stack('win_new'))
    return (y[:SEQ][None], y[SEQ:][:, None], stack('cmp_p'), stack('cmp_s'), stack('sel_p'), stack('sel_s'),
            stack('win_p'), win_s, stack('ssm_p'), ssm_s, stack('conv_p'), stack('conv_s'))
```
